```python
import jax, jax.numpy as jnp
from jax import lax
import numpy as np

D_MODEL = 2048
BATCH = 1
SEQ = 8192
DEPTH = 2

HEAD_DIM = 128
SB_HEADS = 6
SB_BLOCK = 128
DIL_GROUPS = ((128, 1), (512, 4), (2048, 16))
DIL_HEADS_PER_GROUP = 2
DIL_HEADS = DIL_HEADS_PER_GROUP * len(DIL_GROUPS)
GLA_HEADS = 4
GLA_DK = 128
GLA_DV = 256
GLA_GATE_RANK = 16
GLA_GATE_TAU = 16.0
GLA_CHUNK = 64
N_BRANCH = 3
D_FF = -(-8 * D_MODEL // (3 * 256)) * 256
N_MOD = 6
EPS = 1e-6

SB_W = SB_HEADS * HEAD_DIM
DIL_W = DIL_HEADS * HEAD_DIM
DIL_OUT_W = DIL_HEADS_PER_GROUP * HEAD_DIM
GLA_QK_W = GLA_HEADS * GLA_DK
GLA_V_W = GLA_HEADS * GLA_DV
IN_SPLITS = (SB_W, SB_W, SB_W, DIL_W, DIL_W, DIL_W, GLA_QK_W, GLA_QK_W, GLA_V_W, GLA_V_W, GLA_GATE_RANK, N_BRANCH * D_MODEL)
IN_WIDTH = sum(IN_SPLITS)

kernel_name = "hybrid_sb_dilated_gla_adaln_block"


def rms_norm(x, gain):
    xf = x.astype(jnp.float32)
    y = xf * lax.rsqrt(jnp.mean(jnp.square(xf), axis=-1, keepdims=True) + EPS)
    return (y * gain.astype(jnp.float32)).astype(x.dtype)


def split_heads(x, n):
    b, t, w = x.shape
    return x.reshape(b, t, n, w // n).transpose(0, 2, 1, 3)


def merge_heads(x):
    b, h, t, d = x.shape
    return x.transpose(0, 2, 1, 3).reshape(b, t, h * d)


def alibi_slopes(n):
    return 2.0 ** (-8.0 * jnp.arange(1, n + 1, dtype=jnp.float32) / n)


def stick_breaking_attention(q, k, v):
    b, h, t, d = q.shape
    scale = d ** -0.5
    outs = []
    for blk in range(t // SB_BLOCK):
        start = blk * SB_BLOCK
        end = start + SB_BLOCK
        qb = q[:, :, start:end]
        kb = k[:, :, :end]
        vb = v[:, :, :end]
        z = jnp.einsum('bhqd,bhkd->bhqk', qb, kb).astype(jnp.float32) * scale
        t_pos = start + jnp.arange(SB_BLOCK)[:, None]
        s_pos = jnp.arange(end)[None, :]
        causal = s_pos < t_pos
        log_beta = jax.nn.log_sigmoid(z)
        log_rest = jnp.where(causal, jax.nn.log_sigmoid(-z), 0.0)
        after = lax.cumsum(log_rest, axis=3, reverse=True) - log_rest
        w = jnp.where(causal, jnp.exp(log_beta + after), 0.0)
        outs.append(jnp.einsum('bhqk,bhkd->bhqd', w.astype(v.dtype), vb))
    return jnp.concatenate(outs, axis=2)


def dilated_group_attention(q, k, v, window, dilation, slopes):
    b, h, t, d = q.shape
    blk = window // dilation
    u_len = t // dilation
    nb = -(-u_len // blk)
    u_pad = nb * blk

    def strided(x):
        x = x.reshape(b, h, u_len, dilation, d).transpose(0, 1, 3, 2, 4)
        x = jnp.pad(x, ((0, 0), (0, 0), (0, 0), (0, u_pad - u_len), (0, 0)))
        return x.reshape(b, h, dilation, nb, blk, d)

    def with_prev(x):
        prev = jnp.pad(x[:, :, :, :-1], ((0, 0), (0, 0), (0, 0), (1, 0), (0, 0), (0, 0)))
        return jnp.concatenate([prev, x], axis=4)

    qs = strided(q)
    kk = with_prev(strided(k))
    vv = with_prev(strided(v))
    s = jnp.einsum('bhrnqd,bhrnkd->bhrnqk', qs, kk).astype(jnp.float32) * (d ** -0.5)
    i = jnp.arange(blk)[:, None]
    j = jnp.arange(2 * blk)[None, :]
    delta = blk + i - j
    n = jnp.arange(nb)[:, None, None]
    valid = (delta >= 0) & (delta <= blk) & (n * blk - blk + j >= 0)
    bias = -(slopes.astype(jnp.float32)[:, None, None] * (dilation * delta).astype(jnp.float32))
    logits = jnp.where(valid, s + bias[None, :, None, None], -jnp.inf)
    lse = jax.nn.logsumexp(logits, axis=-1)
    p = jnp.exp(logits - lse[..., None])
    o = jnp.einsum('bhrnqk,bhrnkd->bhrnqd', p.astype(v.dtype), vv)
    o = o.reshape(b, h, dilation, u_pad, d)[:, :, :, :u_len].transpose(0, 1, 3, 2, 4).reshape(b, h, t, d)
    lse = lse.reshape(b, h, dilation, u_pad)[..., :u_len].transpose(0, 1, 3, 2).reshape(b, h, t)
    return o, lse


def dilated_mixture(q, k, v, slopes):
    outs, lses = [], []
    for g, (window, dilation) in enumerate(DIL_GROUPS):
        sl = slice(g * DIL_HEADS_PER_GROUP, (g + 1) * DIL_HEADS_PER_GROUP)
        o, l = dilated_group_attention(q[:, sl], k[:, sl], v[:, sl], window, dilation, slopes[sl])
        outs.append(o)
        lses.append(l)
    weights = jax.nn.softmax(jnp.stack(lses, axis=0), axis=0)
    mixed = jnp.sum(weights[..., None] * jnp.stack(outs, axis=0).astype(jnp.float32), axis=0)
    return mixed.astype(q.dtype)


def gla_chunked(q, k, v, log_a):
    b, h, t, dk = q.shape
    dv = v.shape[-1]
    c = GLA_CHUNK
    n = t // c

    def chunks(x):
        return x.reshape(b, h, n, c, x.shape[-1]).transpose(2, 0, 1, 3, 4)

    qc = chunks((q.astype(jnp.float32) * (dk ** -0.5)))
    kc = chunks(k.astype(jnp.float32))
    vc = chunks(v.astype(jnp.float32))
    ac = chunks(log_a.astype(jnp.float32))
    mask = jnp.tril(jnp.ones((c, c), dtype=bool))

    def step(state, inp):
        qi, ki, vi, ai = inp
        cum = jnp.cumsum(ai, axis=2)
        inter = jnp.einsum('bhcd,bhde->bhce', qi * jnp.exp(cum), state)
        diff = cum[:, :, :, None, :] - cum[:, :, None, :, :]
        decay = jnp.exp(jnp.where(mask[:, :, None], diff, -jnp.inf))
        scores = jnp.einsum('bhid,bhjd,bhijd->bhij', qi, ki, decay)
        intra = jnp.einsum('bhij,bhje->bhie', scores, vi)
        last = cum[:, :, -1:, :]
        new_state = jnp.exp(last[:, :, 0, :])[..., None] * state + jnp.einsum('bhcd,bhce->bhde', ki * jnp.exp(last - cum), vi)
        return new_state, inter + intra

    s0 = jnp.zeros((b, h, dk, dv), jnp.float32)
    _, o = lax.scan(step, s0, (qc, kc, vc, ac))
    return o.transpose(1, 2, 0, 3, 4).reshape(b, h, t, dv).astype(v.dtype)


def mixing_sublayer(h, w_in, sb_q_gain, sb_k_gain, dil_q_gain, dil_k_gain, w_gla_a, b_gla_a, gla_out_gain,
                    w_br_sb, w_br_dil, w_br_gla, w_out):
    b, t, d = h.shape
    proj = h @ w_in
    idx = [int(i) for i in np.cumsum(IN_SPLITS)[:-1]]
    (q_sb, k_sb, v_sb, q_dil, k_dil, v_dil, q_gla, k_gla, v_gla, r_gla, a_gla, gate_cols) = jnp.split(proj, idx, axis=-1)

    qa = rms_norm(split_heads(q_sb, SB_HEADS), sb_q_gain)
    ka = rms_norm(split_heads(k_sb, SB_HEADS), sb_k_gain)
    o_sb = merge_heads(stick_breaking_attention(qa, ka, split_heads(v_sb, SB_HEADS)))

    qb = rms_norm(split_heads(q_dil, DIL_HEADS), dil_q_gain)
    kb = rms_norm(split_heads(k_dil, DIL_HEADS), dil_k_gain)
    o_dil = merge_heads(dilated_mixture(qb, kb, split_heads(v_dil, DIL_HEADS), alibi_slopes(DIL_HEADS)))

    a = (a_gla @ w_gla_a + b_gla_a).astype(jnp.float32)
    log_a = jax.nn.log_sigmoid(a) / GLA_GATE_TAU
    o_g = gla_chunked(split_heads(q_gla, GLA_HEADS), split_heads(k_gla, GLA_HEADS),
                      split_heads(v_gla, GLA_HEADS), split_heads(log_a, GLA_HEADS))
    o_gla = merge_heads(rms_norm(o_g, gla_out_gain)) * jax.nn.silu(r_gla)

    g = jax.nn.sigmoid(gate_cols.reshape(b, t, N_BRANCH, d))
    y = (g[:, :, 0] * (o_sb @ w_br_sb)
         + g[:, :, 1] * (o_dil @ w_br_dil)
         + g[:, :, 2] * (o_gla @ w_br_gla))
    return y @ w_out


def swiglu(h, w_ffn_in, w_ffn_out):
    gate, up = jnp.split(h @ w_ffn_in, 2, axis=-1)
    return (jax.nn.silu(gate) * up) @ w_ffn_out


def setup_inputs(seed: int = 0) -> dict:
    key = jax.random.key(seed)
    ks = jax.random.split(key, 20)
    L, D = DEPTH, D_MODEL

    def normal(k, shape, scale):
        return jax.random.normal(k, shape, jnp.float32) * scale

    return {
        "x": normal(ks[0], (BATCH, SEQ, D), 1.0),
        "c": normal(ks[1], (BATCH, D), 1.0),
        "w_ada": normal(ks[2], (L, D, N_MOD * D), 0.5 * D ** -0.5),
        "b_ada": normal(ks[3], (L, N_MOD * D), 0.02),
        "norm1_gain": 1.0 + normal(ks[4], (L, D), 0.02),
        "norm2_gain": 1.0 + normal(ks[5], (L, D), 0.02),
        "w_in": normal(ks[6], (L, D, IN_WIDTH), D ** -0.5),
        "sb_q_gain": 1.0 + normal(ks[7], (L, HEAD_DIM), 0.02),
        "sb_k_gain": 1.0 + normal(ks[8], (L, HEAD_DIM), 0.02),
        "dil_q_gain": 1.0 + normal(ks[9], (L, HEAD_DIM), 0.02),
        "dil_k_gain": 1.0 + normal(ks[10], (L, HEAD_DIM), 0.02),
        "w_gla_a": normal(ks[11], (L, GLA_GATE_RANK, GLA_QK_W), GLA_GATE_RANK ** -0.5),
        "b_gla_a": normal(ks[12], (L, GLA_QK_W), 0.02),
        "gla_out_gain": 1.0 + normal(ks[13], (L, GLA_DV), 0.02),
        "w_br_sb": normal(ks[14], (L, SB_W, D), SB_W ** -0.5),
        "w_br_dil": normal(ks[15], (L, DIL_OUT_W, D), DIL_OUT_W ** -0.5),
        "w_br_gla": normal(ks[16], (L, GLA_V_W, D), GLA_V_W ** -0.5),
        "w_out": normal(ks[17], (L, D, D), D ** -0.5),
        "w_ffn_in": normal(ks[18], (L, D, 2 * D_FF), D ** -0.5),
        "w_ffn_out": normal(ks[19], (L, D_FF, D), D_FF ** -0.5),
    }


def reference(x, c, w_ada, b_ada, norm1_gain, norm2_gain, w_in, sb_q_gain, sb_k_gain, dil_q_gain, dil_k_gain,
              w_gla_a, b_gla_a, gla_out_gain, w_br_sb, w_br_dil, w_br_gla, w_out, w_ffn_in, w_ffn_out):
    for l in range(DEPTH):
        mod = jax.nn.silu(c) @ w_ada[l] + b_ada[l]
        shift1, scale1, gate1, shift2, scale2, gate2 = jnp.split(mod[:, None, :], N_MOD, axis=-1)
        h = rms_norm(x, norm1_gain[l]) * (1 + scale1) + shift1
        x = x + gate1 * mixing_sublayer(h, w_in[l], sb_q_gain[l], sb_k_gain[l], dil_q_gain[l], dil_k_gain[l],
                                        w_gla_a[l], b_gla_a[l], gla_out_gain[l],
                                        w_br_sb[l], w_br_dil[l], w_br_gla[l], w_out[l])
        h = rms_norm(x, norm2_gain[l]) * (1 + scale2) + shift2
        x = x + gate2 * swiglu(h, w_ffn_in[l], w_ffn_out[l])
    return x
```

```python
import functools
import math

import numpy as np
import jax
import jax.numpy as jnp
from jax import lax
from jax.experimental import pallas as pl
from jax.experimental.pallas import tpu as pltpu

F32 = jnp.float32
BF16 = jnp.bfloat16

D_MODEL = 2048
SEQ = 8192
DEPTH = 2
HEAD_DIM = 128
SB_HEADS = 6
DIL_GROUPS = ((128, 1), (512, 4), (2048, 16))
DIL_HEADS_PER_GROUP = 2
DIL_HEADS = DIL_HEADS_PER_GROUP * len(DIL_GROUPS)
GLA_HEADS = 4
GLA_DK = 128
GLA_DV = 256
GLA_GATE_RANK = 16
GLA_GATE_TAU = 16.0
N_BRANCH = 3
D_FF = 5632
N_MOD = 6
EPS = 1e-6

SB_W = SB_HEADS * HEAD_DIM
DIL_W = DIL_HEADS * HEAD_DIM
DIL_OUT_W = DIL_HEADS_PER_GROUP * HEAD_DIM
GLA_QK_W = GLA_HEADS * GLA_DK
GLA_V_W = GLA_HEADS * GLA_DV
OFF_SB = 0
OFF_DIL = 3 * SB_W
OFF_GLA = OFF_DIL + 3 * DIL_W
OFF_A = OFF_GLA + 2 * GLA_QK_W + 2 * GLA_V_W
OFF_GATE = OFF_A + GLA_GATE_RANK
MAIN_W = OFF_A

LANES = 128
VMEM_LIMIT = 56 * 1024 * 1024

MM_TM = 512
IN_TN = 768
SB_T = 256
DIL_BLK = 128
GLA_L = 128
GLA_LEVELS = 7


def _params(*sem):
    return pltpu.CompilerParams(dimension_semantics=sem, vmem_limit_bytes=VMEM_LIMIT)


def _dot(a, b):
    return jnp.dot(a, b, preferred_element_type=F32)


def _dot_nt(a, b):
    return lax.dot_general(a, b, (((1,), (1,)), ((), ())), preferred_element_type=F32)


def _split2(x):
    hi = x.astype(BF16)
    lo = (x - hi.astype(F32)).astype(BF16)
    return hi, lo


def _split3(x):
    p1 = x.astype(BF16)
    r1 = x - p1.astype(F32)
    p2 = r1.astype(BF16)
    p3 = (r1 - p2.astype(F32)).astype(BF16)
    return p1, p2, p3


def _log_sigmoid(z):
    return jnp.minimum(z, 0.0) - jnp.log(1.0 + jnp.exp(-jnp.abs(z)))


def _ada_kernel(c_ref, w_ref, b_ref, o_ref):
    c = c_ref[...]
    s = jnp.broadcast_to(c * jax.nn.sigmoid(c), (8, D_MODEL))
    s_hi, s_lo = _split2(s)
    w_hi, w_lo = _split2(w_ref[...])
    acc = _dot(s_hi, w_hi) + _dot(s_lo, w_hi) + _dot(s_hi, w_lo)
    o_ref[...] = acc[0:1] + b_ref[...]


def _ada(c, w_ada, b_ada):
    tn = 1024
    n = N_MOD * D_MODEL
    return pl.pallas_call(
        _ada_kernel,
        grid=(DEPTH, n // tn),
        in_specs=[
            pl.BlockSpec((1, D_MODEL), lambda l, j: (0, 0)),
            pl.BlockSpec((None, D_MODEL, tn), lambda l, j: (l, 0, j)),
            pl.BlockSpec((None, 1, tn), lambda l, j: (l, 0, j)),
        ],
        out_specs=pl.BlockSpec((None, 1, tn), lambda l, j: (l, 0, j)),
        out_shape=jax.ShapeDtypeStruct((DEPTH, 1, n), F32),
        compiler_params=_params("arbitrary", "arbitrary"),
        name="ada_mod",
    )(c, w_ada, b_ada.reshape(DEPTH, 1, n))


def _norm_kernel(x_ref, g_ref, sc_ref, sh_ref, o_ref):
    x = x_ref[...]
    ms = jnp.mean(x * x, axis=-1, keepdims=True)
    y = x * lax.rsqrt(ms + EPS) * g_ref[...]
    o_ref[...] = (y * (1.0 + sc_ref[...]) + sh_ref[...]).astype(BF16)


def _norm_mod(x, gain, mod, l, shift_idx, scale_idx):
    tm = 512
    return pl.pallas_call(
        _norm_kernel,
        grid=(SEQ // tm,),
        in_specs=[
            pl.BlockSpec((tm, D_MODEL), lambda i: (i, 0)),
            pl.BlockSpec((None, 1, D_MODEL), lambda i: (l, 0, 0)),
            pl.BlockSpec((None, 1, D_MODEL), lambda i: (l, 0, scale_idx)),
            pl.BlockSpec((None, 1, D_MODEL), lambda i: (l, 0, shift_idx)),
        ],
        out_specs=pl.BlockSpec((tm, D_MODEL), lambda i: (i, 0)),
        out_shape=jax.ShapeDtypeStruct((SEQ, D_MODEL), BF16),
        compiler_params=_params("arbitrary"),
        name="norm_mod",
    )(x, gain.reshape(DEPTH, 1, D_MODEL), mod, mod)


def _cast_weights(pairs):
    @pl.when(pl.program_id(1) == 0)
    def _():
        for w_ref, wb_ref in pairs:
            wb_ref[...] = w_ref[...].astype(BF16)


def _inproj_kernel(a_ref, w_ref, g_ref, o_ref, wb_ref):
    _cast_weights([(w_ref, wb_ref)])
    j = pl.program_id(0)
    acc = _dot(a_ref[...], wb_ref[...])
    is_norm = (j == 0) | (j == 1) | (j == 3) | (j == 4)

    @pl.when(is_norm)
    def _():
        g = g_ref[...]
        for c in range(IN_TN // HEAD_DIM):
            cs = slice(c * HEAD_DIM, (c + 1) * HEAD_DIM)
            blk = acc[:, cs]
            ms = jnp.mean(blk * blk, axis=-1, keepdims=True)
            o_ref[:, cs] = (blk * lax.rsqrt(ms + EPS) * g[:, cs]).astype(BF16)

    @pl.when(jnp.logical_not(is_norm))
    def _():
        o_ref[...] = acc.astype(BF16)


def _inproj_main(h, w_in, l, gains):
    tm, tn = MM_TM, IN_TN
    return pl.pallas_call(
        _inproj_kernel,
        grid=(MAIN_W // tn, SEQ // tm),
        in_specs=[
            pl.BlockSpec((tm, D_MODEL), lambda j, m: (m, 0)),
            pl.BlockSpec((None, D_MODEL, tn), lambda j, m: (l, 0, j)),
            pl.BlockSpec((1, tn), lambda j, m: (0, j)),
        ],
        out_specs=pl.BlockSpec((tm, tn), lambda j, m: (m, j)),
        out_shape=jax.ShapeDtypeStruct((SEQ, MAIN_W), BF16),
        scratch_shapes=[pltpu.VMEM((D_MODEL, tn), BF16)],
        compiler_params=_params("arbitrary", "arbitrary"),
        name="inproj_main",
    )(h, w_in, gains)


def _plain_kernel(a_ref, w_ref, o_ref, wb_ref, *, sigmoid):
    _cast_weights([(w_ref, wb_ref)])
    acc = _dot(a_ref[...], wb_ref[...])
    if sigmoid:
        acc = jax.nn.sigmoid(acc)
    o_ref[...] = acc.astype(o_ref.dtype)


def _inproj_a(h, w_in, l):
    tm, tn = MM_TM, LANES
    return pl.pallas_call(
        functools.partial(_plain_kernel, sigmoid=False),
        grid=(1, SEQ // tm),
        in_specs=[
            pl.BlockSpec((tm, D_MODEL), lambda j, m: (m, 0)),
            pl.BlockSpec((None, D_MODEL, tn), lambda j, m: (l, 0, OFF_A // tn)),
        ],
        out_specs=pl.BlockSpec((tm, tn), lambda j, m: (m, 0)),
        out_shape=jax.ShapeDtypeStruct((SEQ, tn), F32),
        scratch_shapes=[pltpu.VMEM((D_MODEL, tn), BF16)],
        compiler_params=_params("arbitrary", "arbitrary"),
        name="inproj_a",
    )(h, w_in)


def _inproj_gates(h, w_gate, l):
    tm, tn = MM_TM, IN_TN
    n = N_BRANCH * D_MODEL
    return pl.pallas_call(
        functools.partial(_plain_kernel, sigmoid=True),
        grid=(n // tn, SEQ // tm),
        in_specs=[
            pl.BlockSpec((tm, D_MODEL), lambda j, m: (m, 0)),
            pl.BlockSpec((None, D_MODEL, tn), lambda j, m: (l, 0, j)),
        ],
        out_specs=pl.BlockSpec((tm, tn), lambda j, m: (m, j)),
        out_shape=jax.ShapeDtypeStruct((SEQ, n), BF16),
        scratch_shapes=[pltpu.VMEM((D_MODEL, tn), BF16)],
        compiler_params=_params("arbitrary", "arbitrary"),
        name="inproj_gates",
    )(h, w_gate)


def _branch_kernel(a0, a1, a2, w0, w1, w2, g0, g1, g2, o_ref, wb0, wb1, wb2):
    _cast_weights([(w0, wb0), (w1, wb1), (w2, wb2)])
    y = g0[...].astype(F32) * _dot(a0[...], wb0[...])
    y = y + g1[...].astype(F32) * _dot(a1[...], wb1[...])
    y = y + g2[...].astype(F32) * _dot(a2[...], wb2[...])
    o_ref[...] = y.astype(BF16)


def _branch(o_sb, o_dil, o_gla, w_br_sb, w_br_dil, w_br_gla, gates, l):
    tm, tn = MM_TM, 512
    nb = D_MODEL // tn
    a_spec = lambda w: pl.BlockSpec((tm, w), lambda j, m: (m, 0))
    w_spec = lambda w: pl.BlockSpec((None, w, tn), lambda j, m: (l, 0, j))
    g_spec = lambda b: pl.BlockSpec((tm, tn), lambda j, m: (m, b * nb + j))
    return pl.pallas_call(
        _branch_kernel,
        grid=(nb, SEQ // tm),
        in_specs=[a_spec(SB_W), a_spec(DIL_OUT_W), a_spec(GLA_V_W),
                  w_spec(SB_W), w_spec(DIL_OUT_W), w_spec(GLA_V_W),
                  g_spec(0), g_spec(1), g_spec(2)],
        out_specs=pl.BlockSpec((tm, tn), lambda j, m: (m, j)),
        out_shape=jax.ShapeDtypeStruct((SEQ, D_MODEL), BF16),
        scratch_shapes=[pltpu.VMEM((SB_W, tn), BF16), pltpu.VMEM((DIL_OUT_W, tn), BF16),
                        pltpu.VMEM((GLA_V_W, tn), BF16)],
        compiler_params=_params("arbitrary", "arbitrary"),
        name="branch_proj",
    )(o_sb, o_dil, o_gla, w_br_sb, w_br_dil, w_br_gla, gates, gates, gates)


def _resid_kernel(a_ref, w_ref, x_ref, gate_ref, o_ref, wb_ref):
    _cast_weights([(w_ref, wb_ref)])
    o_ref[...] = x_ref[...] + gate_ref[...] * _dot(a_ref[...], wb_ref[...])


def _resid_proj(a, w, x, mod, l, gate_idx, tn):
    tm = MM_TM
    k = a.shape[1]
    nb = D_MODEL // tn
    return pl.pallas_call(
        _resid_kernel,
        grid=(nb, SEQ // tm),
        in_specs=[
            pl.BlockSpec((tm, k), lambda j, m: (m, 0)),
            pl.BlockSpec((None, k, tn), lambda j, m: (l, 0, j)),
            pl.BlockSpec((tm, tn), lambda j, m: (m, j)),
            pl.BlockSpec((None, 1, tn), lambda j, m: (l, 0, gate_idx * nb + j)),
        ],
        out_specs=pl.BlockSpec((tm, tn), lambda j, m: (m, j)),
        out_shape=jax.ShapeDtypeStruct((SEQ, D_MODEL), F32),
        scratch_shapes=[pltpu.VMEM((k, tn), BF16)],
        compiler_params=_params("arbitrary", "arbitrary"),
        name="resid_proj",
    )(a, w, x, mod)


def _ffn_in_kernel(a_ref, wg_ref, wu_ref, o_ref, wgb_ref, wub_ref):
    _cast_weights([(wg_ref, wgb_ref), (wu_ref, wub_ref)])
    a = a_ref[...]
    g = _dot(a, wgb_ref[...])
    u = _dot(a, wub_ref[...])
    o_ref[...] = (g * jax.nn.sigmoid(g) * u).astype(BF16)


def _ffn_in(h, w_ffn_in, l):
    tm, tn = MM_TM, 512
    nb = D_FF // tn
    return pl.pallas_call(
        _ffn_in_kernel,
        grid=(nb, SEQ // tm),
        in_specs=[
            pl.BlockSpec((tm, D_MODEL), lambda j, m: (m, 0)),
            pl.BlockSpec((None, D_MODEL, tn), lambda j, m: (l, 0, j)),
            pl.BlockSpec((None, D_MODEL, tn), lambda j, m: (l, 0, nb + j)),
        ],
        out_specs=pl.BlockSpec((tm, tn), lambda j, m: (m, j)),
        out_shape=jax.ShapeDtypeStruct((SEQ, D_FF), BF16),
        scratch_shapes=[pltpu.VMEM((D_MODEL, tn), BF16), pltpu.VMEM((D_MODEL, tn), BF16)],
        compiler_params=_params("arbitrary", "arbitrary"),
        name="ffn_in",
    )(h, w_ffn_in, w_ffn_in)


def _sb_kernel(q_ref, k_ref, v_ref, u_ref, o_ref):
    t = SB_T
    i = pl.program_id(1)
    q = q_ref[...]
    u = u_ref[...]

    def tile(kb, carry, acc, diag):
        ks = pl.multiple_of(kb * t, t)
        k = k_ref[pl.ds(ks, t), :]
        v = v_ref[pl.ds(ks, t), :]
        z = _dot_nt(q, k)
        ls = _log_sigmoid(z)
        lr = ls - z
        if diag:
            row = lax.broadcasted_iota(jnp.int32, (t, t), 0)
            col = lax.broadcasted_iota(jnp.int32, (t, t), 1)
            causal = col < row
            lr = jnp.where(causal, lr, 0.0)
        hi, lo = _split2(lr)
        after = _dot(hi, u) + _dot(lo, u) + carry
        w = jnp.exp(ls + after)
        if diag:
            w = jnp.where(causal, w, 0.0)
        acc = acc + _dot(w.astype(BF16), v)
        carry = carry + jnp.sum(lr, axis=-1, keepdims=True)
        return carry, acc

    carry0 = jnp.zeros((t, 1), F32)
    acc0 = jnp.zeros((t, HEAD_DIM), F32)
    carry, acc = tile(i, carry0, acc0, True)

    def body(j, c):
        return tile(i - 1 - j, c[0], c[1], False)

    carry, acc = lax.fori_loop(0, i, body, (carry, acc))
    o_ref[...] = acc.astype(BF16)


def _sb_attention(p, u_tri):
    t = SB_T
    qb, kb, vb = OFF_SB // HEAD_DIM, (OFF_SB + SB_W) // HEAD_DIM, (OFF_SB + 2 * SB_W) // HEAD_DIM
    return pl.pallas_call(
        _sb_kernel,
        grid=(SB_HEADS, SEQ // t),
        in_specs=[
            pl.BlockSpec((t, HEAD_DIM), lambda h, i: (i, qb + h)),
            pl.BlockSpec((SEQ, HEAD_DIM), lambda h, i: (0, kb + h)),
            pl.BlockSpec((SEQ, HEAD_DIM), lambda h, i: (0, vb + h)),
            pl.BlockSpec((t, t), lambda h, i: (0, 0)),
        ],
        out_specs=pl.BlockSpec((t, HEAD_DIM), lambda h, i: (i, h)),
        out_shape=jax.ShapeDtypeStruct((SEQ, SB_W), BF16),
        compiler_params=_params("arbitrary", "arbitrary"),
        name="sb_attention",
    )(p, p, p, u_tri)


def _dil_kernel(q_ref, kc_ref, kp_ref, vc_ref, vp_ref, o_ref, l_ref, *, bq, dilation, slopes):
    n = pl.program_id(1)
    blk = DIL_BLK
    i_idx = lax.broadcasted_iota(jnp.int32, (blk, 2 * blk), 0)
    j_idx = lax.broadcasted_iota(jnp.int32, (blk, 2 * blk), 1)
    delta = blk + i_idx - j_idx
    in_window = (delta >= 0) & (delta <= blk)
    first_valid = in_window & (j_idx >= jnp.where(n > 0, 0, blk))
    delta_f = delta.astype(F32)
    for hh in range(DIL_HEADS_PER_GROUP):
        cs = slice(hh * HEAD_DIM, (hh + 1) * HEAD_DIM)
        bias = (-slopes[hh] * dilation) * delta_f
        for b in range(bq // blk):
            rs = slice(b * blk, (b + 1) * blk)
            q = q_ref[rs, cs]
            if b == 0:
                k_prev, v_prev = kp_ref[:, cs], vp_ref[:, cs]
            else:
                ps = slice((b - 1) * blk, b * blk)
                k_prev, v_prev = kc_ref[ps, cs], vc_ref[ps, cs]
            kk = jnp.concatenate([k_prev, kc_ref[rs, cs]], axis=0)
            vv = jnp.concatenate([v_prev, vc_ref[rs, cs]], axis=0)
            s = _dot_nt(q, kk)
            valid = first_valid if b == 0 else in_window
            logits = jnp.where(valid, s + bias, -jnp.inf)
            m = jnp.max(logits, axis=-1, keepdims=True)
            pexp = jnp.exp(logits - m)
            den = jnp.sum(pexp, axis=-1, keepdims=True)
            o = _dot(pexp.astype(BF16), vv) / den
            o_ref[rs, cs] = o
            l_ref[rs, cs] = jnp.broadcast_to(m + jnp.log(den), (blk, HEAD_DIM))


def _dil_group(p, g):
    _, r = DIL_GROUPS[g]
    rows = SEQ // r
    bq = min(rows, 512)
    sub = bq // DIL_BLK
    wb = MAIN_W // DIL_OUT_W
    qb = OFF_DIL // DIL_OUT_W + g
    kb = (OFF_DIL + DIL_W) // DIL_OUT_W + g
    vb = (OFF_DIL + 2 * DIL_W) // DIL_OUT_W + g
    slopes = tuple(2.0 ** (-8.0 * (g * DIL_HEADS_PER_GROUP + hh + 1) / DIL_HEADS)
                   for hh in range(DIL_HEADS_PER_GROUP))
    pv = p.reshape(rows, r * MAIN_W)
    cur = lambda c: pl.BlockSpec((bq, DIL_OUT_W), lambda rho, n: (n, rho * wb + c))
    prev = lambda c: pl.BlockSpec((DIL_BLK, DIL_OUT_W),
                                  lambda rho, n: (jnp.maximum(n * sub - 1, 0), rho * wb + c))
    out_spec = pl.BlockSpec((bq, DIL_OUT_W), lambda rho, n: (n, rho))
    o, lse = pl.pallas_call(
        functools.partial(_dil_kernel, bq=bq, dilation=r, slopes=slopes),
        grid=(r, rows // bq),
        in_specs=[cur(qb), cur(kb), prev(kb), cur(vb), prev(vb)],
        out_specs=[out_spec, out_spec],
        out_shape=[jax.ShapeDtypeStruct((rows, r * DIL_OUT_W), F32)] * 2,
        compiler_params=_params("arbitrary", "arbitrary"),
        name=f"dil_group{g}",
    )(pv, pv, pv, pv, pv)
    return o.reshape(SEQ, DIL_OUT_W), lse.reshape(SEQ, DIL_OUT_W)


def _dil_mix_kernel(o0, o1, o2, l0, l1, l2, out_ref):
    a, b, c = l0[...], l1[...], l2[...]
    m = jnp.maximum(jnp.maximum(a, b), c)
    ea, eb, ec = jnp.exp(a - m), jnp.exp(b - m), jnp.exp(c - m)
    mixed = (ea * o0[...] + eb * o1[...] + ec * o2[...]) / (ea + eb + ec)
    out_ref[...] = mixed.astype(BF16)


def _dil_mix(outs, lses):
    tm = 1024
    spec = pl.BlockSpec((tm, DIL_OUT_W), lambda i: (i, 0))
    return pl.pallas_call(
        _dil_mix_kernel,
        grid=(SEQ // tm,),
        in_specs=[spec] * 6,
        out_specs=spec,
        out_shape=jax.ShapeDtypeStruct((SEQ, DIL_OUT_W), BF16),
        compiler_params=_params("arbitrary"),
        name="dil_mix",
    )(*outs, *lses)


def _gla_sum_matrix():
    L = GLA_L
    t = np.arange(L)[:, None]
    s = np.arange(L)[None, :]
    mats = []
    for lev in range(GLA_LEVELS):
        size = 2 << lev
        half = size // 2
        mid = (t // size) * size + half
        later = (t % size) >= half
        mats.append(np.where(later, (s >= mid) & (s <= t), (s > t) & (s < mid)))
    mats.append(s <= t)
    mats.append(s > t)
    return np.concatenate(mats, axis=0).astype(np.float32)


def _gla_kernel(q_ref, k_ref, v0_ref, v1_ref, r0_ref, r1_ref, a_ref, wa_ref, ba_ref, gain_ref, mst_ref,
                o_ref, st_ref):
    L = GLA_L

    @pl.when(pl.program_id(0) == 0)
    def _():
        st_ref[...] = jnp.zeros_like(st_ref)

    row = lax.broadcasted_iota(jnp.int32, (L, L), 0)
    col = lax.broadcasted_iota(jnp.int32, (L, L), 1)
    a_hi, a_lo = _split2(a_ref[:, 0:GLA_GATE_RANK])
    mst = mst_ref[...]
    gain = gain_ref[...]
    for h in range(GLA_HEADS):
        ks = slice(h * GLA_DK, (h + 1) * GLA_DK)
        wa_hi, wa_lo = _split2(wa_ref[:, ks])
        a = _dot(a_hi, wa_hi) + _dot(a_lo, wa_hi) + _dot(a_hi, wa_lo) + ba_ref[:, ks]
        log_a = _log_sigmoid(a) * (1.0 / GLA_GATE_TAU)
        p1, p2, p3 = _split3(log_a)
        e_all = _dot(mst, p1) + _dot(mst, p2) + _dot(mst, p3)
        qb = q_ref[:, ks]
        kb = k_ref[:, ks]
        q = qb.astype(F32)
        k = kb.astype(F32)
        scores = jnp.where(row == col, _dot_nt(qb, kb), 0.0)
        for lev in range(GLA_LEVELS):
            size = 2 << lev
            e = jnp.exp(e_all[lev * L:(lev + 1) * L])
            later = (row & (size // 2)) != 0
            qt = jnp.where(later, q * e, 0.0).astype(BF16)
            kt = jnp.where(later, 0.0, k * e).astype(BF16)
            part = _dot_nt(qt, kt)
            if size < L:
                part = jnp.where((row >> (lev + 1)) == (col >> (lev + 1)), part, 0.0)
            scores = scores + part
        e_cum = e_all[GLA_LEVELS * L:(GLA_LEVELS + 1) * L]
        e_rev = e_all[(GLA_LEVELS + 1) * L:(GLA_LEVELS + 2) * L]
        total = e_cum[L - 1:L]
        q_in = (q * jnp.exp(e_cum)).astype(BF16)
        k_out = (k * jnp.exp(e_rev)).astype(BF16)
        v_src = v0_ref if h < 2 else v1_ref
        r_src = r0_ref if h < 2 else r1_ref
        vs = slice((h % 2) * GLA_DV, (h % 2 + 1) * GLA_DV)
        v = v_src[:, vs]
        st = st_ref[h]
        o = _dot_nt(q_in, st.astype(BF16)) + _dot(scores.astype(BF16), v)
        o = o * (GLA_DK ** -0.5)
        v_t = v.astype(F32).T.astype(BF16)
        st_ref[h] = st * jnp.exp(total) + _dot(v_t, k_out)
        ms = jnp.mean(o * o, axis=-1, keepdims=True)
        y = o * lax.rsqrt(ms + EPS) * gain
        r = r_src[:, vs].astype(F32)
        o_ref[:, h * GLA_DV:(h + 1) * GLA_DV] = (y * (r * jax.nn.sigmoid(r))).astype(BF16)


def _gla(p, a_cols, w_gla_a, b_gla_a, gla_out_gain, mst, l):
    L = GLA_L
    w = 512
    base = OFF_GLA // w
    col = lambda c: pl.BlockSpec((L, w), lambda i: (i, base + c))
    return pl.pallas_call(
        _gla_kernel,
        grid=(SEQ // L,),
        in_specs=[col(0), col(1), col(2), col(3), col(4), col(5),
                  pl.BlockSpec((L, LANES), lambda i: (i, 0)),
                  pl.BlockSpec((None, GLA_GATE_RANK, GLA_QK_W), lambda i: (l, 0, 0)),
                  pl.BlockSpec((None, 1, GLA_QK_W), lambda i: (l, 0, 0)),
                  pl.BlockSpec((None, 1, GLA_DV), lambda i: (l, 0, 0)),
                  pl.BlockSpec(((GLA_LEVELS + 2) * L, L), lambda i: (0, 0))],
        out_specs=pl.BlockSpec((L, GLA_V_W), lambda i: (i, 0)),
        out_shape=jax.ShapeDtypeStruct((SEQ, GLA_V_W), BF16),
        scratch_shapes=[pltpu.VMEM((GLA_HEADS, GLA_DV, GLA_DK), F32)],
        compiler_params=_params("arbitrary"),
        name="gla",
    )(p, p, p, p, p, p, a_cols, w_gla_a, b_gla_a.reshape(DEPTH, 1, GLA_QK_W),
      gla_out_gain.reshape(DEPTH, 1, GLA_DV), mst)


def _qk_gains(sb_q_gain, sb_k_gain, dil_q_gain, dil_k_gain):
    scale = HEAD_DIM ** -0.5
    ones = jnp.ones((SB_W,), F32)
    tail = jnp.ones((MAIN_W - OFF_GLA,), F32)
    return jnp.concatenate([
        jnp.tile(sb_q_gain * scale, SB_HEADS), jnp.tile(sb_k_gain, SB_HEADS), ones,
        jnp.tile(dil_q_gain * scale, DIL_HEADS), jnp.tile(dil_k_gain, DIL_HEADS), ones, tail,
    ]).reshape(1, MAIN_W)


def kernel(x, c, w_ada, b_ada, norm1_gain, norm2_gain, w_in, sb_q_gain, sb_k_gain, dil_q_gain, dil_k_gain,
           w_gla_a, b_gla_a, gla_out_gain, w_br_sb, w_br_dil, w_br_gla, w_out, w_ffn_in, w_ffn_out):
    assert x.shape == (1, SEQ, D_MODEL) and c.shape == (1, D_MODEL)
    xs = x.reshape(SEQ, D_MODEL)
    mod = _ada(c, w_ada, b_ada)
    w_gate = w_in[:, :, OFF_GATE:]
    u_tri = jnp.asarray(np.tril(np.ones((SB_T, SB_T), np.float32), -1), BF16)
    mst = jnp.asarray(_gla_sum_matrix(), BF16)
    for l in range(DEPTH):
        gains = _qk_gains(sb_q_gain[l], sb_k_gain[l], dil_q_gain[l], dil_k_gain[l])
        h = _norm_mod(xs, norm1_gain, mod, l, 0, 1)
        p = _inproj_main(h, w_in, l, gains)
        a_cols = _inproj_a(h, w_in, l)
        gates = _inproj_gates(h, w_gate, l)
        o_sb = _sb_attention(p, u_tri)
        dil = [_dil_group(p, g) for g in range(len(DIL_GROUPS))]
        o_dil = _dil_mix([d[0] for d in dil], [d[1] for d in dil])
        o_gla = _gla(p, a_cols, w_gla_a, b_gla_a, gla_out_gain, mst, l)
        y = _branch(o_sb, o_dil, o_gla, w_br_sb, w_br_dil, w_br_gla, gates, l)
        xs = _resid_proj(y, w_out, xs, mod, l, 2, 512)
        h2 = _norm_mod(xs, norm2_gain, mod, l, 3, 4)
        u = _ffn_in(h2, w_ffn_in, l)
        xs = _resid_proj(u, w_ffn_out, xs, mod, l, 5, 256)
    return xs.reshape(1, SEQ, D_MODEL)
```

```python
import functools
import math

import numpy as np
import jax
import jax.numpy as jnp
from jax import lax
from jax.experimental import pallas as pl
from jax.experimental.pallas import tpu as pltpu

F32 = jnp.float32
BF16 = jnp.bfloat16

D_MODEL = 2048
SEQ = 8192
DEPTH = 2
HEAD_DIM = 128
SB_HEADS = 6
DIL_GROUPS = ((128, 1), (512, 4), (2048, 16))
DIL_HEADS_PER_GROUP = 2
DIL_HEADS = DIL_HEADS_PER_GROUP * len(DIL_GROUPS)
GLA_HEADS = 4
GLA_DK = 128
GLA_DV = 256
GLA_GATE_RANK = 16
GLA_GATE_TAU = 16.0
N_BRANCH = 3
D_FF = 5632
N_MOD = 6
EPS = 1e-6

SB_W = SB_HEADS * HEAD_DIM
DIL_W = DIL_HEADS * HEAD_DIM
DIL_OUT_W = DIL_HEADS_PER_GROUP * HEAD_DIM
GLA_QK_W = GLA_HEADS * GLA_DK
GLA_V_W = GLA_HEADS * GLA_DV
OFF_SB = 0
OFF_DIL = 3 * SB_W
OFF_GLA = OFF_DIL + 3 * DIL_W
OFF_A = OFF_GLA + 2 * GLA_QK_W + 2 * GLA_V_W
OFF_GATE = OFF_A + GLA_GATE_RANK
SEG_SB_W = 3 * SB_W
SEG_DIL_W = 3 * DIL_W
SEG_GLA_W = 2 * GLA_QK_W + 2 * GLA_V_W

LANES = 128
VMEM_LIMIT = 56 * 1024 * 1024

MM_TM = 512
IN_TN = 768
SB_T = 256
SB_G = 2
SB_QS = 2
LOG2E = 1.4426950408889634
DIL_BLK = 128
GLA_L = 128
GLA_LEVELS = 7


def _params(*sem):
    return pltpu.CompilerParams(dimension_semantics=sem, vmem_limit_bytes=VMEM_LIMIT)


def _dot(a, b):
    return jnp.dot(a, b, preferred_element_type=F32)


def _dot_nt(a, b):
    return lax.dot_general(a, b, (((1,), (1,)), ((), ())), preferred_element_type=F32)


def _split2(x):
    hi = x.astype(BF16)
    lo = (x - hi.astype(F32)).astype(BF16)
    return hi, lo


def _split3(x):
    p1 = x.astype(BF16)
    r1 = x - p1.astype(F32)
    p2 = r1.astype(BF16)
    p3 = (r1 - p2.astype(F32)).astype(BF16)
    return p1, p2, p3


def _log_sigmoid(z):
    return jnp.minimum(z, 0.0) - jnp.log(1.0 + jnp.exp(-jnp.abs(z)))


def _ada_kernel(c_ref, w_ref, b_ref, o_ref):
    c = c_ref[...]
    s = jnp.broadcast_to(c * jax.nn.sigmoid(c), (8, D_MODEL))
    s_hi, s_lo = _split2(s)
    w_hi, w_lo = _split2(w_ref[...])
    acc = _dot(s_hi, w_hi) + _dot(s_lo, w_hi) + _dot(s_hi, w_lo)
    o_ref[...] = acc[0:1] + b_ref[...]


def _ada(c, w_ada, b_ada):
    tn = 1024
    n = N_MOD * D_MODEL
    return pl.pallas_call(
        _ada_kernel,
        grid=(DEPTH, n // tn),
        in_specs=[
            pl.BlockSpec((1, D_MODEL), lambda l, j: (0, 0)),
            pl.BlockSpec((None, D_MODEL, tn), lambda l, j: (l, 0, j)),
            pl.BlockSpec((None, 1, tn), lambda l, j: (l, 0, j)),
        ],
        out_specs=pl.BlockSpec((None, 1, tn), lambda l, j: (l, 0, j)),
        out_shape=jax.ShapeDtypeStruct((DEPTH, 1, n), F32),
        compiler_params=_params("arbitrary", "arbitrary"),
        name="ada_mod",
    )(c, w_ada, b_ada.reshape(DEPTH, 1, n))


def _norm_kernel(x_ref, g_ref, sc_ref, sh_ref, o_ref):
    x = x_ref[...]
    ms = jnp.mean(x * x, axis=-1, keepdims=True)
    y = x * lax.rsqrt(ms + EPS) * g_ref[...]
    o_ref[...] = (y * (1.0 + sc_ref[...]) + sh_ref[...]).astype(BF16)


def _norm_mod(x, gain, mod, l, shift_idx, scale_idx):
    tm = 512
    return pl.pallas_call(
        _norm_kernel,
        grid=(SEQ // tm,),
        in_specs=[
            pl.BlockSpec((tm, D_MODEL), lambda i: (i, 0)),
            pl.BlockSpec((None, 1, D_MODEL), lambda i: (l, 0, 0)),
            pl.BlockSpec((None, 1, D_MODEL), lambda i: (l, 0, scale_idx)),
            pl.BlockSpec((None, 1, D_MODEL), lambda i: (l, 0, shift_idx)),
        ],
        out_specs=pl.BlockSpec((tm, D_MODEL), lambda i: (i, 0)),
        out_shape=jax.ShapeDtypeStruct((SEQ, D_MODEL), BF16),
        compiler_params=_params("arbitrary"),
        name="norm_mod",
    )(x, gain.reshape(DEPTH, 1, D_MODEL), mod, mod)


def _cast_weights(pairs):
    @pl.when(pl.program_id(1) == 0)
    def _():
        for w_ref, wb_ref in pairs:
            wb_ref[...] = w_ref[...].astype(BF16)


def _inproj_qkv_kernel(a_ref, w_ref, g_ref, o_ref, wb_ref):
    _cast_weights([(w_ref, wb_ref)])
    acc = _dot(a_ref[...], wb_ref[...])
    is_norm = pl.program_id(0) < 2

    @pl.when(is_norm)
    def _():
        g = g_ref[...]
        for c in range(IN_TN // HEAD_DIM):
            cs = slice(c * HEAD_DIM, (c + 1) * HEAD_DIM)
            blk = acc[:, cs]
            ms = jnp.mean(blk * blk, axis=-1, keepdims=True)
            o_ref[:, cs] = (blk * lax.rsqrt(ms + EPS) * g[:, cs]).astype(o_ref.dtype)

    @pl.when(jnp.logical_not(is_norm))
    def _():
        o_ref[...] = acc.astype(o_ref.dtype)


def _inproj_qkv(h, w_in, l, col0, gains, out_dtype, name):
    tm, tn = MM_TM, IN_TN
    width = 3 * tn
    return pl.pallas_call(
        _inproj_qkv_kernel,
        grid=(width // tn, SEQ // tm),
        in_specs=[
            pl.BlockSpec((tm, D_MODEL), lambda j, m: (m, 0)),
            pl.BlockSpec((None, D_MODEL, tn), lambda j, m: (l, 0, col0 // tn + j)),
            pl.BlockSpec((1, tn), lambda j, m: (0, j)),
        ],
        out_specs=pl.BlockSpec((tm, tn), lambda j, m: (m, j)),
        out_shape=jax.ShapeDtypeStruct((SEQ, width), out_dtype),
        scratch_shapes=[pltpu.VMEM((D_MODEL, tn), BF16)],
        compiler_params=_params("arbitrary", "arbitrary"),
        name=name,
    )(h, w_in, gains)


def _plain_kernel(a_ref, w_ref, o_ref, wb_ref):
    _cast_weights([(w_ref, wb_ref)])
    o_ref[...] = _dot(a_ref[...], wb_ref[...]).astype(o_ref.dtype)


def _inproj_plain(h, w_in, l, col0, width, tn, out_dtype, name):
    tm = MM_TM
    return pl.pallas_call(
        _plain_kernel,
        grid=(width // tn, SEQ // tm),
        in_specs=[
            pl.BlockSpec((tm, D_MODEL), lambda j, m: (m, 0)),
            pl.BlockSpec((None, D_MODEL, tn), lambda j, m: (l, 0, col0 // tn + j)),
        ],
        out_specs=pl.BlockSpec((tm, tn), lambda j, m: (m, j)),
        out_shape=jax.ShapeDtypeStruct((SEQ, width), out_dtype),
        scratch_shapes=[pltpu.VMEM((D_MODEL, tn), BF16)],
        compiler_params=_params("arbitrary", "arbitrary"),
        name=name,
    )(h, w_in)


GATE_SHIFT = OFF_GATE - OFF_A
GATE_ROWS = 256


def _gates_kernel(a_ref, wm_ref, wx_ref, o_ref, wb_ref):
    tn = IN_TN

    @pl.when(pl.program_id(1) == 0)
    def _():
        for r in range(D_MODEL // GATE_ROWS):
            rs = slice(r * GATE_ROWS, (r + 1) * GATE_ROWS)
            both = jnp.concatenate([wm_ref[rs, :], wx_ref[rs, :]], axis=1)
            wb_ref[rs, :] = both[:, GATE_SHIFT:GATE_SHIFT + tn].astype(BF16)

    o_ref[...] = jax.nn.sigmoid(_dot(a_ref[...], wb_ref[...])).astype(BF16)


def _inproj_gates(h, w_in, l):
    tm, tn = MM_TM, IN_TN
    n = N_BRANCH * D_MODEL
    return pl.pallas_call(
        _gates_kernel,
        grid=(n // tn, SEQ // tm),
        in_specs=[
            pl.BlockSpec((tm, D_MODEL), lambda j, m: (m, 0)),
            pl.BlockSpec((None, D_MODEL, tn), lambda j, m: (l, 0, OFF_A // tn + j)),
            pl.BlockSpec((None, D_MODEL, LANES), lambda j, m: (l, 0, (OFF_A + (j + 1) * tn) // LANES)),
        ],
        out_specs=pl.BlockSpec((tm, tn), lambda j, m: (m, j)),
        out_shape=jax.ShapeDtypeStruct((SEQ, n), BF16),
        scratch_shapes=[pltpu.VMEM((D_MODEL, tn), BF16)],
        compiler_params=_params("arbitrary", "arbitrary"),
        name="inproj_gates",
    )(h, w_in, w_in)


def _branch_kernel(a0, a1, a2, w0, w1, w2, g0, g1, g2, o_ref, wb0, wb1, wb2):
    _cast_weights([(w0, wb0), (w1, wb1), (w2, wb2)])
    y = g0[...].astype(F32) * _dot(a0[...], wb0[...])
    y = y + g1[...].astype(F32) * _dot(a1[...], wb1[...])
    y = y + g2[...].astype(F32) * _dot(a2[...], wb2[...])
    o_ref[...] = y.astype(BF16)


def _branch(o_sb, o_dil, o_gla, w_br_sb, w_br_dil, w_br_gla, gates, l):
    tm, tn = MM_TM, 512
    nb = D_MODEL // tn
    a_spec = lambda w: pl.BlockSpec((tm, w), lambda j, m: (m, 0))
    w_spec = lambda w: pl.BlockSpec((None, w, tn), lambda j, m: (l, 0, j))
    g_spec = lambda b: pl.BlockSpec((tm, tn), lambda j, m: (m, b * nb + j))
    return pl.pallas_call(
        _branch_kernel,
        grid=(nb, SEQ // tm),
        in_specs=[a_spec(SB_W), a_spec(DIL_OUT_W), a_spec(GLA_V_W),
                  w_spec(SB_W), w_spec(DIL_OUT_W), w_spec(GLA_V_W),
                  g_spec(0), g_spec(1), g_spec(2)],
        out_specs=pl.BlockSpec((tm, tn), lambda j, m: (m, j)),
        out_shape=jax.ShapeDtypeStruct((SEQ, D_MODEL), BF16),
        scratch_shapes=[pltpu.VMEM((SB_W, tn), BF16), pltpu.VMEM((DIL_OUT_W, tn), BF16),
                        pltpu.VMEM((GLA_V_W, tn), BF16)],
        compiler_params=_params("arbitrary", "arbitrary"),
        name="branch_proj",
    )(o_sb, o_dil, o_gla, w_br_sb, w_br_dil, w_br_gla, gates, gates, gates)


def _resid_kernel(a_ref, w_ref, x_ref, gate_ref, o_ref, wb_ref):
    _cast_weights([(w_ref, wb_ref)])
    o_ref[...] = x_ref[...] + gate_ref[...] * _dot(a_ref[...], wb_ref[...])


def _resid_proj(a, w, x, mod, l, gate_idx, tn):
    tm = MM_TM
    k = a.shape[1]
    nb = D_MODEL // tn
    return pl.pallas_call(
        _resid_kernel,
        grid=(nb, SEQ // tm),
        in_specs=[
            pl.BlockSpec((tm, k), lambda j, m: (m, 0)),
            pl.BlockSpec((None, k, tn), lambda j, m: (l, 0, j)),
            pl.BlockSpec((tm, tn), lambda j, m: (m, j)),
            pl.BlockSpec((None, 1, tn), lambda j, m: (l, 0, gate_idx * nb + j)),
        ],
        out_specs=pl.BlockSpec((tm, tn), lambda j, m: (m, j)),
        out_shape=jax.ShapeDtypeStruct((SEQ, D_MODEL), F32),
        scratch_shapes=[pltpu.VMEM((k, tn), BF16)],
        compiler_params=_params("arbitrary", "arbitrary"),
        name="resid_proj",
    )(a, w, x, mod)


def _ffn_in_kernel(a_ref, wg_ref, wu_ref, o_ref, wgb_ref, wub_ref):
    _cast_weights([(wg_ref, wgb_ref), (wu_ref, wub_ref)])
    a = a_ref[...]
    g = _dot(a, wgb_ref[...])
    u = _dot(a, wub_ref[...])
    o_ref[...] = (g * jax.nn.sigmoid(g) * u).astype(BF16)


def _ffn_in(h, w_ffn_in, l):
    tm, tn = MM_TM, 512
    nb = D_FF // tn
    return pl.pallas_call(
        _ffn_in_kernel,
        grid=(nb, SEQ // tm),
        in_specs=[
            pl.BlockSpec((tm, D_MODEL), lambda j, m: (m, 0)),
            pl.BlockSpec((None, D_MODEL, tn), lambda j, m: (l, 0, j)),
            pl.BlockSpec((None, D_MODEL, tn), lambda j, m: (l, 0, nb + j)),
        ],
        out_specs=pl.BlockSpec((tm, tn), lambda j, m: (m, j)),
        out_shape=jax.ShapeDtypeStruct((SEQ, D_FF), BF16),
        scratch_shapes=[pltpu.VMEM((D_MODEL, tn), BF16), pltpu.VMEM((D_MODEL, tn), BF16)],
        compiler_params=_params("arbitrary", "arbitrary"),
        name="ffn_in",
    )(h, w_ffn_in, w_ffn_in)


def _sb_kernel(q_ref, k_ref, v_ref, u_ref, o_ref, z_ref, acc_ref, carry_ref):
    t = SB_T
    i = pl.program_id(1)
    u = u_ref[...]
    units = [(s, g) for s in range(SB_QS) for g in range(SB_G)]

    def q_rows(s):
        return slice(s * t, (s + 1) * t)

    def cols(g):
        return slice(g * HEAD_DIM, (g + 1) * HEAD_DIM)

    def logits_into(kb, slot, subs):
        ks = pl.multiple_of(kb * t, t)
        for n, (s, g) in enumerate(units):
            if s in subs:
                z_ref[slot, n] = _dot_nt(q_ref[q_rows(s), cols(g)], k_ref[pl.ds(ks, t), cols(g)])

    def tile(kb, slot, kinds):
        ks = pl.multiple_of(kb * t, t)
        if 'diag' in kinds:
            row = lax.broadcasted_iota(jnp.int32, (t, t), 0)
            col = lax.broadcasted_iota(jnp.int32, (t, t), 1)
            causal = col < row
        for n, (s, g) in enumerate(units):
            if kinds[s] is None:
                continue
            diag = kinds[s] == 'diag'
            z = z_ref[slot, n]
            neg_abs = pltpu.bitcast(pltpu.bitcast(z, jnp.uint32) | jnp.uint32(0x80000000), F32)
            ls = jnp.minimum(z, 0.0) - jnp.log(1.0 + jnp.exp2(neg_abs)) * LOG2E
            lr = ls - z
            if diag:
                lr = jnp.where(causal, lr, 0.0)
            carry = carry_ref[n]
            after = _dot(lr.astype(BF16), u) + jnp.concatenate([carry] * (t // LANES), axis=1)
            w = jnp.exp2(ls + after)
            if diag:
                w = jnp.where(causal, w, 0.0)
            acc_ref[n] += _dot(w.astype(BF16), v_ref[pl.ds(ks, t), cols(g)])
            carry_ref[n] = carry + jnp.sum(lr, axis=-1, keepdims=True)

    acc_ref[...] = jnp.zeros_like(acc_ref)
    carry_ref[...] = jnp.zeros_like(carry_ref)
    all_subs = tuple(range(SB_QS))
    top = SB_QS * i + SB_QS - 1
    logits_into(top, 0, (SB_QS - 1,))
    for d in range(SB_QS):
        kinds = [None if s < SB_QS - 1 - d else ('diag' if s == SB_QS - 1 - d else 'full') for s in all_subs]
        tile(top - d, d % 2, kinds)
        nxt = tuple(s for s in all_subs if s >= SB_QS - 2 - d) if d < SB_QS - 1 else all_subs
        logits_into(jnp.maximum(top - d - 1, 0), (d + 1) % 2, nxt)

    assert SB_QS % 2 == 0

    def body(j, _):
        kb = SB_QS * i - 1 - 2 * j
        for half in range(2):
            tile(kb - half, half, ['full'] * SB_QS)
            logits_into(jnp.maximum(kb - half - 1, 0), 1 - half, all_subs)
        return 0

    lax.fori_loop(0, (SB_QS // 2) * i, body, 0)
    for n, (s, g) in enumerate(units):
        o_ref[q_rows(s), cols(g)] = acc_ref[n].astype(BF16)


def _sb_attention(p, u_tri):
    t = SB_T
    tq = SB_QS * t
    w = SB_G * HEAD_DIM
    ng = SB_HEADS // SB_G
    nu = SB_QS * SB_G
    return pl.pallas_call(
        _sb_kernel,
        grid=(ng, SEQ // tq),
        in_specs=[
            pl.BlockSpec((tq, w), lambda h, i: (i, h)),
            pl.BlockSpec((SEQ, w), lambda h, i: (0, ng + h)),
            pl.BlockSpec((SEQ, w), lambda h, i: (0, 2 * ng + h)),
            pl.BlockSpec((t, t), lambda h, i: (0, 0)),
        ],
        out_specs=pl.BlockSpec((tq, w), lambda h, i: (i, h)),
        out_shape=jax.ShapeDtypeStruct((SEQ, SB_W), BF16),
        scratch_shapes=[pltpu.VMEM((2, nu, t, t), F32), pltpu.VMEM((nu, t, HEAD_DIM), F32),
                        pltpu.VMEM((nu, t, LANES), F32)],
        compiler_params=_params("arbitrary", "arbitrary"),
        name="sb_attention",
    )(p, p, p, u_tri)


DIL_ROWS = 2048


def _stream_rows(start, dilation):
    if dilation == 1:
        return pl.ds(start, DIL_BLK)
    return pl.ds(start, DIL_BLK, stride=dilation)


def _dil_kernel(*refs, dilation, slopes):
    nh = DIL_HEADS_PER_GROUP
    ins = [refs[5 * hh:5 * hh + 5] for hh in range(nh)]
    o_refs = refs[5 * nh:6 * nh]
    l_refs = refs[6 * nh:7 * nh]
    n = pl.program_id(0)
    blk, r = DIL_BLK, dilation
    i_idx = lax.broadcasted_iota(jnp.int32, (blk, 2 * blk), 0)
    j_idx = lax.broadcasted_iota(jnp.int32, (blk, 2 * blk), 1)
    delta = blk + i_idx - j_idx
    in_window = (delta >= 0) & (delta <= blk)
    first_valid = in_window & (j_idx >= jnp.where(n > 0, 0, blk))
    delta_f = delta.astype(F32)
    biases = [(-slopes[hh] * r) * delta_f for hh in range(DIL_HEADS_PER_GROUP)]
    for rho in range(r):
        for b in range(DIL_ROWS // (r * blk)):
            cur = _stream_rows(rho + r * b * blk, r)
            prev = _stream_rows(rho + r * max(b - 1, 0) * blk, r)
            valid = first_valid if b == 0 else in_window
            for hh in range(nh):
                q_ref, kc_ref, kp_ref, vc_ref, vp_ref = ins[hh]
                k_prev = kp_ref[prev, :] if b == 0 else kc_ref[prev, :]
                v_prev = vp_ref[prev, :] if b == 0 else vc_ref[prev, :]
                kk = jnp.concatenate([k_prev, kc_ref[cur, :]], axis=0).astype(BF16)
                vv = jnp.concatenate([v_prev, vc_ref[cur, :]], axis=0).astype(BF16)
                s = _dot_nt(q_ref[cur, :].astype(BF16), kk)
                logits = jnp.where(valid, s + biases[hh], -jnp.inf)
                m = jnp.max(logits, axis=-1, keepdims=True)
                pexp = jnp.exp(logits - m)
                den = jnp.sum(pexp, axis=-1, keepdims=True)
                o_refs[hh][cur, :] = _dot(pexp.astype(BF16), vv) / den
                l_refs[hh][cur, :] = jnp.broadcast_to(m + jnp.log(den), (blk, HEAD_DIM))


def _dil_group(p_dil, g):
    _, r = DIL_GROUPS[g]
    prev_rows = r * DIL_BLK
    ratio = DIL_ROWS // prev_rows
    nh = DIL_HEADS_PER_GROUP
    slopes = tuple(2.0 ** (-8.0 * (g * nh + hh + 1) / DIL_HEADS) for hh in range(nh))
    cur = lambda part, hh: pl.BlockSpec((DIL_ROWS, HEAD_DIM), lambda n: (n, part * DIL_HEADS + g * nh + hh))
    prev = lambda part, hh: pl.BlockSpec((prev_rows, HEAD_DIM),
                                         lambda n: (jnp.maximum(n * ratio - 1, 0), part * DIL_HEADS + g * nh + hh))
    in_specs = []
    for hh in range(nh):
        in_specs += [cur(0, hh), cur(1, hh), prev(1, hh), cur(2, hh), prev(2, hh)]
    out_spec = pl.BlockSpec((DIL_ROWS, HEAD_DIM), lambda n: (n, 0))
    res = pl.pallas_call(
        functools.partial(_dil_kernel, dilation=r, slopes=slopes),
        grid=(SEQ // DIL_ROWS,),
        in_specs=in_specs,
        out_specs=[out_spec] * (2 * nh),
        out_shape=[jax.ShapeDtypeStruct((SEQ, HEAD_DIM), F32)] * (2 * nh),
        compiler_params=_params("arbitrary"),
        name=f"dil_group{g}",
    )(*([p_dil] * (5 * nh)))
    return res[:nh], res[nh:]


def _dil_mix_kernel(*refs):
    ng, nh = len(DIL_GROUPS), DIL_HEADS_PER_GROUP
    out_ref = refs[-1]
    for hh in range(nh):
        o = [refs[g * nh + hh][...] for g in range(ng)]
        lse = [refs[ng * nh + g * nh + hh][...] for g in range(ng)]
        m = functools.reduce(jnp.maximum, lse)
        e = [jnp.exp(x - m) for x in lse]
        num = functools.reduce(lambda a, b: a + b, [ei * oi for ei, oi in zip(e, o)])
        den = functools.reduce(lambda a, b: a + b, e)
        out_ref[:, hh * HEAD_DIM:(hh + 1) * HEAD_DIM] = (num / den).astype(BF16)


def _dil_mix(outs, lses):
    tm = 1024
    flat = [a for grp in outs for a in grp] + [a for grp in lses for a in grp]
    return pl.pallas_call(
        _dil_mix_kernel,
        grid=(SEQ // tm,),
        in_specs=[pl.BlockSpec((tm, HEAD_DIM), lambda i: (i, 0))] * len(flat),
        out_specs=pl.BlockSpec((tm, DIL_OUT_W), lambda i: (i, 0)),
        out_shape=jax.ShapeDtypeStruct((SEQ, DIL_OUT_W), BF16),
        compiler_params=_params("arbitrary"),
        name="dil_mix",
    )(*flat)


def _gla_sum_matrix():
    L = GLA_L
    t = np.arange(L)[:, None]
    s = np.arange(L)[None, :]
    mats = []
    for lev in range(GLA_LEVELS):
        size = 2 << lev
        half = size // 2
        mid = (t // size) * size + half
        later = (t % size) >= half
        mats.append(np.where(later, (s >= mid) & (s <= t), (s > t) & (s < mid)))
    mats.append(s <= t)
    mats.append(s > t)
    return np.concatenate(mats, axis=0).astype(np.float32)


def _gla_kernel(q_ref, k_ref, v0_ref, v1_ref, r0_ref, r1_ref, a_ref, wa_ref, ba_ref, gain_ref, mst_ref,
                o_ref, st_ref):
    L = GLA_L

    @pl.when(pl.program_id(0) == 0)
    def _():
        st_ref[...] = jnp.zeros_like(st_ref)

    row = lax.broadcasted_iota(jnp.int32, (L, L), 0)
    col = lax.broadcasted_iota(jnp.int32, (L, L), 1)
    a_hi, a_lo = _split2(a_ref[:, 0:GLA_GATE_RANK])
    mst = mst_ref[...]
    gain = gain_ref[...]
    for h in range(GLA_HEADS):
        ks = slice(h * GLA_DK, (h + 1) * GLA_DK)
        wa_hi, wa_lo = _split2(wa_ref[:, ks])
        a = _dot(a_hi, wa_hi) + _dot(a_lo, wa_hi) + _dot(a_hi, wa_lo) + ba_ref[:, ks]
        log_a = _log_sigmoid(a) * (1.0 / GLA_GATE_TAU)
        p1, p2 = _split2(log_a)
        e_all = _dot(mst, p1) + _dot(mst, p2)
        qb = q_ref[:, ks]
        kb = k_ref[:, ks]
        q = qb.astype(F32)
        k = kb.astype(F32)
        scores = jnp.where(row == col, _dot_nt(qb, kb), 0.0)
        for lev in range(GLA_LEVELS):
            size = 2 << lev
            e = jnp.exp(e_all[lev * L:(lev + 1) * L])
            later = (row & (size // 2)) != 0
            qt = jnp.where(later, q * e, 0.0).astype(BF16)
            kt = jnp.where(later, 0.0, k * e).astype(BF16)
            part = _dot_nt(qt, kt)
            if size < L:
                part = jnp.where((row >> (lev + 1)) == (col >> (lev + 1)), part, 0.0)
            scores = scores + part
        e_cum = e_all[GLA_LEVELS * L:(GLA_LEVELS + 1) * L]
        e_rev = e_all[(GLA_LEVELS + 1) * L:(GLA_LEVELS + 2) * L]
        total = e_cum[L - 1:L]
        q_in = (q * jnp.exp(e_cum)).astype(BF16)
        k_out = (k * jnp.exp(e_rev)).astype(BF16)
        v_src = v0_ref if h < 2 else v1_ref
        r_src = r0_ref if h < 2 else r1_ref
        vs = slice((h % 2) * GLA_DV, (h % 2 + 1) * GLA_DV)
        v = v_src[:, vs]
        st = st_ref[h]
        o = _dot_nt(q_in, st.astype(BF16)) + _dot(scores.astype(BF16), v)
        o = o * (GLA_DK ** -0.5)
        v_t = v.astype(F32).T.astype(BF16)
        st_ref[h] = st * jnp.exp(total) + _dot(v_t, k_out)
        ms = jnp.mean(o * o, axis=-1, keepdims=True)
        y = o * lax.rsqrt(ms + EPS) * gain
        r = r_src[:, vs].astype(F32)
        o_ref[:, h * GLA_DV:(h + 1) * GLA_DV] = (y * (r * jax.nn.sigmoid(r))).astype(BF16)


def _gla(p, a_cols, w_gla_a, b_gla_a, gla_out_gain, mst, l):
    L = GLA_L
    w = 512
    col = lambda c: pl.BlockSpec((L, w), lambda i: (i, c))
    return pl.pallas_call(
        _gla_kernel,
        grid=(SEQ // L,),
        in_specs=[col(0), col(1), col(2), col(3), col(4), col(5),
                  pl.BlockSpec((L, LANES), lambda i: (i, 0)),
                  pl.BlockSpec((None, GLA_GATE_RANK, GLA_QK_W), lambda i: (l, 0, 0)),
                  pl.BlockSpec((None, 1, GLA_QK_W), lambda i: (l, 0, 0)),
                  pl.BlockSpec((None, 1, GLA_DV), lambda i: (l, 0, 0)),
                  pl.BlockSpec(((GLA_LEVELS + 2) * L, L), lambda i: (0, 0))],
        out_specs=pl.BlockSpec((L, GLA_V_W), lambda i: (i, 0)),
        out_shape=jax.ShapeDtypeStruct((SEQ, GLA_V_W), BF16),
        scratch_shapes=[pltpu.VMEM((GLA_HEADS, GLA_DV, GLA_DK), F32)],
        compiler_params=_params("arbitrary"),
        name="gla",
    )(p, p, p, p, p, p, a_cols, w_gla_a, b_gla_a.reshape(DEPTH, 1, GLA_QK_W),
      gla_out_gain.reshape(DEPTH, 1, GLA_DV), mst)


def _qk_gains(q_gain, k_gain, heads, q_scale):
    ones = jnp.ones((heads * HEAD_DIM,), F32)
    return jnp.concatenate([jnp.tile(q_gain * q_scale, heads), jnp.tile(k_gain, heads), ones]).reshape(1, -1)


def kernel(x, c, w_ada, b_ada, norm1_gain, norm2_gain, w_in, sb_q_gain, sb_k_gain, dil_q_gain, dil_k_gain,
           w_gla_a, b_gla_a, gla_out_gain, w_br_sb, w_br_dil, w_br_gla, w_out, w_ffn_in, w_ffn_out):
    assert x.shape == (1, SEQ, D_MODEL) and c.shape == (1, D_MODEL)
    xs = x.reshape(SEQ, D_MODEL)
    mod = _ada(c, w_ada, b_ada)
    u_tri = jnp.asarray(np.tril(np.ones((SB_T, SB_T), np.float32), -1), BF16)
    mst = jnp.asarray(_gla_sum_matrix(), BF16)
    scale = HEAD_DIM ** -0.5
    for l in range(DEPTH):
        h = _norm_mod(xs, norm1_gain, mod, l, 0, 1)
        p_sb = _inproj_qkv(h, w_in, l, OFF_SB, _qk_gains(sb_q_gain[l], sb_k_gain[l], SB_HEADS, scale * LOG2E),
                           BF16, "inproj_sb")
        p_dil = _inproj_qkv(h, w_in, l, OFF_DIL, _qk_gains(dil_q_gain[l], dil_k_gain[l], DIL_HEADS, scale),
                            F32, "inproj_dil")
        p = _inproj_plain(h, w_in, l, OFF_GLA, SEG_GLA_W, IN_TN, BF16, "inproj_gla")
        a_cols = _inproj_plain(h, w_in, l, OFF_A, LANES, LANES, F32, "inproj_a")
        gates = _inproj_gates(h, w_in, l)
        o_sb = _sb_attention(p_sb, u_tri)
        dil = [_dil_group(p_dil, g) for g in range(len(DIL_GROUPS))]
        o_dil = _dil_mix([d[0] for d in dil], [d[1] for d in dil])
        o_gla = _gla(p, a_cols, w_gla_a, b_gla_a, gla_out_gain, mst, l)
        y = _branch(o_sb, o_dil, o_gla, w_br_sb, w_br_dil, w_br_gla, gates, l)
        xs = _resid_proj(y, w_out, xs, mod, l, 2, 512)
        h2 = _norm_mod(xs, norm2_gain, mod, l, 3, 4)
        u = _ffn_in(h2, w_ffn_in, l)
        xs = _resid_proj(u, w_ffn_out, xs, mod, l, 5, 256)
    return xs.reshape(1, SEQ, D_MODEL)
```

```python
import functools
import math

import numpy as np
import jax
import jax.numpy as jnp
from jax import lax
from jax.experimental import pallas as pl
from jax.experimental.pallas import tpu as pltpu

F32 = jnp.float32
BF16 = jnp.bfloat16

D_MODEL = 2048
SEQ = 8192
DEPTH = 2
HEAD_DIM = 128
SB_HEADS = 6
DIL_GROUPS = ((128, 1), (512, 4), (2048, 16))
DIL_HEADS_PER_GROUP = 2
DIL_HEADS = DIL_HEADS_PER_GROUP * len(DIL_GROUPS)
GLA_HEADS = 4
GLA_DK = 128
GLA_DV = 256
GLA_GATE_RANK = 16
GLA_GATE_TAU = 16.0
N_BRANCH = 3
D_FF = 5632
N_MOD = 6
EPS = 1e-6

SB_W = SB_HEADS * HEAD_DIM
DIL_W = DIL_HEADS * HEAD_DIM
DIL_OUT_W = DIL_HEADS_PER_GROUP * HEAD_DIM
GLA_QK_W = GLA_HEADS * GLA_DK
GLA_V_W = GLA_HEADS * GLA_DV
OFF_SB = 0
OFF_DIL = 3 * SB_W
OFF_GLA = OFF_DIL + 3 * DIL_W
OFF_A = OFF_GLA + 2 * GLA_QK_W + 2 * GLA_V_W
OFF_GATE = OFF_A + GLA_GATE_RANK
SEG_SB_W = 3 * SB_W
SEG_DIL_W = 3 * DIL_W
SEG_GLA_W = 2 * GLA_QK_W + 2 * GLA_V_W

LANES = 128
VMEM_LIMIT = 56 * 1024 * 1024

MM_TM = 1024
IN_TN = 768
SB_T = 256
SB_G = 2
SB_QS = 2
LOG2E = 1.4426950408889634
DIL_BLK = 128
GLA_L = 128
GLA_NB = 2
GLA_LEVELS = 7


def _params(*sem):
    return pltpu.CompilerParams(dimension_semantics=sem, vmem_limit_bytes=VMEM_LIMIT)


def _dot(a, b):
    return jnp.dot(a, b, preferred_element_type=F32)


def _dot_nt(a, b):
    return lax.dot_general(a, b, (((1,), (1,)), ((), ())), preferred_element_type=F32)


def _split2(x):
    hi = x.astype(BF16)
    lo = (x - hi.astype(F32)).astype(BF16)
    return hi, lo


def _split3(x):
    p1 = x.astype(BF16)
    r1 = x - p1.astype(F32)
    p2 = r1.astype(BF16)
    p3 = (r1 - p2.astype(F32)).astype(BF16)
    return p1, p2, p3


def _log_sigmoid(z):
    return jnp.minimum(z, 0.0) - jnp.log(1.0 + jnp.exp(-jnp.abs(z)))


def _ada_kernel(c_ref, w_ref, b_ref, o_ref):
    c = c_ref[...]
    s = jnp.broadcast_to(c * jax.nn.sigmoid(c), (8, D_MODEL))
    s_hi, s_lo = _split2(s)
    w_hi, w_lo = _split2(w_ref[...])
    acc = _dot(s_hi, w_hi) + _dot(s_lo, w_hi) + _dot(s_hi, w_lo)
    o_ref[...] = acc[0:1] + b_ref[...]


def _ada(c, w_ada, b_ada):
    tn = 1024
    n = N_MOD * D_MODEL
    return pl.pallas_call(
        _ada_kernel,
        grid=(DEPTH, n // tn),
        in_specs=[
            pl.BlockSpec((1, D_MODEL), lambda l, j: (0, 0)),
            pl.BlockSpec((None, D_MODEL, tn), lambda l, j: (l, 0, j)),
            pl.BlockSpec((None, 1, tn), lambda l, j: (l, 0, j)),
        ],
        out_specs=pl.BlockSpec((None, 1, tn), lambda l, j: (l, 0, j)),
        out_shape=jax.ShapeDtypeStruct((DEPTH, 1, n), F32),
        compiler_params=_params("arbitrary", "arbitrary"),
        name="ada_mod",
    )(c, w_ada, b_ada.reshape(DEPTH, 1, n))


def _norm_kernel(x_ref, g_ref, sc_ref, sh_ref, o_ref):
    x = x_ref[...]
    ms = jnp.mean(x * x, axis=-1, keepdims=True)
    y = x * lax.rsqrt(ms + EPS) * g_ref[...]
    o_ref[...] = (y * (1.0 + sc_ref[...]) + sh_ref[...]).astype(BF16)


def _norm_mod(x, gain, mod, l, shift_idx, scale_idx):
    tm = 512
    return pl.pallas_call(
        _norm_kernel,
        grid=(SEQ // tm,),
        in_specs=[
            pl.BlockSpec((tm, D_MODEL), lambda i: (i, 0)),
            pl.BlockSpec((None, 1, D_MODEL), lambda i: (l, 0, 0)),
            pl.BlockSpec((None, 1, D_MODEL), lambda i: (l, 0, scale_idx)),
            pl.BlockSpec((None, 1, D_MODEL), lambda i: (l, 0, shift_idx)),
        ],
        out_specs=pl.BlockSpec((tm, D_MODEL), lambda i: (i, 0)),
        out_shape=jax.ShapeDtypeStruct((SEQ, D_MODEL), BF16),
        compiler_params=_params("arbitrary"),
        name="norm_mod",
    )(x, gain.reshape(DEPTH, 1, D_MODEL), mod, mod)


def _cast_weights(pairs):
    @pl.when(pl.program_id(1) == 0)
    def _():
        for w_ref, wb_ref in pairs:
            wb_ref[...] = w_ref[...].astype(BF16)


XPOSE_CHUNK = 256


def _cast_weights_t(wt_ref, wb_ref, skip=0, tail_ref=None):
    @pl.when(pl.program_id(1) == 0)
    def _():
        for c in range(D_MODEL // XPOSE_CHUNK):
            cs = slice(c * XPOSE_CHUNK, (c + 1) * XPOSE_CHUNK)
            blk = wt_ref[skip:, cs]
            if tail_ref is not None:
                blk = jnp.concatenate([blk, tail_ref[:, cs]], axis=0)
            wb_ref[cs, :] = blk.T.astype(BF16)


def _inproj_qkv_kernel(a_ref, w_ref, g_ref, o_ref, wb_ref):
    _cast_weights_t(w_ref, wb_ref)
    acc = _dot(a_ref[...], wb_ref[...])
    is_norm = pl.program_id(0) < 2

    @pl.when(is_norm)
    def _():
        g = g_ref[...]
        for c in range(IN_TN // HEAD_DIM):
            cs = slice(c * HEAD_DIM, (c + 1) * HEAD_DIM)
            blk = acc[:, cs]
            ms = jnp.mean(blk * blk, axis=-1, keepdims=True)
            o_ref[:, cs] = (blk * lax.rsqrt(ms + EPS) * g[:, cs]).astype(o_ref.dtype)

    @pl.when(jnp.logical_not(is_norm))
    def _():
        o_ref[...] = acc.astype(o_ref.dtype)


def _inproj_qkv(h, w_in_t, l, col0, gains, out_dtype, name):
    tm, tn = MM_TM, IN_TN
    width = 3 * tn
    return pl.pallas_call(
        _inproj_qkv_kernel,
        grid=(width // tn, SEQ // tm),
        in_specs=[
            pl.BlockSpec((tm, D_MODEL), lambda j, m: (m, 0)),
            pl.BlockSpec((None, tn, D_MODEL), lambda j, m: (l, col0 // tn + j, 0)),
            pl.BlockSpec((1, tn), lambda j, m: (0, j)),
        ],
        out_specs=pl.BlockSpec((tm, tn), lambda j, m: (m, j)),
        out_shape=jax.ShapeDtypeStruct((SEQ, width), out_dtype),
        scratch_shapes=[pltpu.VMEM((D_MODEL, tn), BF16)],
        compiler_params=_params("arbitrary", "arbitrary"),
        name=name,
    )(h, w_in_t, gains)


def _plain_kernel(a_ref, w_ref, o_ref, wb_ref):
    _cast_weights_t(w_ref, wb_ref)
    o_ref[...] = _dot(a_ref[...], wb_ref[...]).astype(o_ref.dtype)


def _inproj_plain(h, w_in_t, l, col0, width, tn, out_dtype, name):
    tm = MM_TM
    return pl.pallas_call(
        _plain_kernel,
        grid=(width // tn, SEQ // tm),
        in_specs=[
            pl.BlockSpec((tm, D_MODEL), lambda j, m: (m, 0)),
            pl.BlockSpec((None, tn, D_MODEL), lambda j, m: (l, col0 // tn + j, 0)),
        ],
        out_specs=pl.BlockSpec((tm, tn), lambda j, m: (m, j)),
        out_shape=jax.ShapeDtypeStruct((SEQ, width), out_dtype),
        scratch_shapes=[pltpu.VMEM((D_MODEL, tn), BF16)],
        compiler_params=_params("arbitrary", "arbitrary"),
        name=name,
    )(h, w_in_t)


GATE_SHIFT = OFF_GATE - OFF_A


def _gates_kernel(a_ref, wm_ref, wx_ref, o_ref, wb_ref):
    _cast_weights_t(wm_ref, wb_ref, skip=GATE_SHIFT, tail_ref=wx_ref)
    o_ref[...] = jax.nn.sigmoid(_dot(a_ref[...], wb_ref[...])).astype(BF16)


def _inproj_gates(h, w_in_t, l):
    tm, tn = MM_TM, IN_TN
    n = N_BRANCH * D_MODEL
    return pl.pallas_call(
        _gates_kernel,
        grid=(n // tn, SEQ // tm),
        in_specs=[
            pl.BlockSpec((tm, D_MODEL), lambda j, m: (m, 0)),
            pl.BlockSpec((None, tn, D_MODEL), lambda j, m: (l, OFF_A // tn + j, 0)),
            pl.BlockSpec((None, GATE_SHIFT, D_MODEL), lambda j, m: (l, (OFF_A + (j + 1) * tn) // GATE_SHIFT, 0)),
        ],
        out_specs=pl.BlockSpec((tm, tn), lambda j, m: (m, j)),
        out_shape=jax.ShapeDtypeStruct((SEQ, n), BF16),
        scratch_shapes=[pltpu.VMEM((D_MODEL, tn), BF16)],
        compiler_params=_params("arbitrary", "arbitrary"),
        name="inproj_gates",
    )(h, w_in_t, w_in_t)


def _branch_kernel(a0, a1, a2, w0, w1, w2, g0, g1, g2, o_ref, wb0, wb1, wb2):
    _cast_weights([(w0, wb0), (w1, wb1), (w2, wb2)])
    y = g0[...].astype(F32) * _dot(a0[...], wb0[...])
    y = y + g1[...].astype(F32) * _dot(a1[...], wb1[...])
    y = y + g2[...].astype(F32) * _dot(a2[...], wb2[...])
    o_ref[...] = y.astype(BF16)


def _branch(o_sb, o_dil, o_gla, w_br_sb, w_br_dil, w_br_gla, gates, l):
    tm, tn = MM_TM, 512
    nb = D_MODEL // tn
    a_spec = lambda w: pl.BlockSpec((tm, w), lambda j, m: (m, 0))
    w_spec = lambda w: pl.BlockSpec((None, w, tn), lambda j, m: (l, 0, j))
    g_spec = lambda b: pl.BlockSpec((tm, tn), lambda j, m: (m, b * nb + j))
    return pl.pallas_call(
        _branch_kernel,
        grid=(nb, SEQ // tm),
        in_specs=[a_spec(SB_W), a_spec(DIL_OUT_W), a_spec(GLA_V_W),
                  w_spec(SB_W), w_spec(DIL_OUT_W), w_spec(GLA_V_W),
                  g_spec(0), g_spec(1), g_spec(2)],
        out_specs=pl.BlockSpec((tm, tn), lambda j, m: (m, j)),
        out_shape=jax.ShapeDtypeStruct((SEQ, D_MODEL), BF16),
        scratch_shapes=[pltpu.VMEM((SB_W, tn), BF16), pltpu.VMEM((DIL_OUT_W, tn), BF16),
                        pltpu.VMEM((GLA_V_W, tn), BF16)],
        compiler_params=_params("arbitrary", "arbitrary"),
        name="branch_proj",
    )(o_sb, o_dil, o_gla, w_br_sb, w_br_dil, w_br_gla, gates, gates, gates)


def _resid_kernel(a_ref, w_ref, x_ref, gate_ref, o_ref, wb_ref):
    _cast_weights([(w_ref, wb_ref)])
    o_ref[...] = x_ref[...] + gate_ref[...] * _dot(a_ref[...], wb_ref[...])


def _resid_proj(a, w, x, mod, l, gate_idx, tm, tn):
    k = a.shape[1]
    nb = D_MODEL // tn
    return pl.pallas_call(
        _resid_kernel,
        grid=(nb, SEQ // tm),
        in_specs=[
            pl.BlockSpec((tm, k), lambda j, m: (m, 0)),
            pl.BlockSpec((None, k, tn), lambda j, m: (l, 0, j)),
            pl.BlockSpec((tm, tn), lambda j, m: (m, j)),
            pl.BlockSpec((None, 1, tn), lambda j, m: (l, 0, gate_idx * nb + j)),
        ],
        out_specs=pl.BlockSpec((tm, tn), lambda j, m: (m, j)),
        out_shape=jax.ShapeDtypeStruct((SEQ, D_MODEL), F32),
        scratch_shapes=[pltpu.VMEM((k, tn), BF16)],
        compiler_params=_params("arbitrary", "arbitrary"),
        name="resid_proj",
    )(a, w, x, mod)


def _ffn_in_kernel(a_ref, wg_ref, wu_ref, o_ref, wgb_ref, wub_ref):
    _cast_weights([(wg_ref, wgb_ref), (wu_ref, wub_ref)])
    a = a_ref[...]
    g = _dot(a, wgb_ref[...])
    u = _dot(a, wub_ref[...])
    o_ref[...] = (g * jax.nn.sigmoid(g) * u).astype(BF16)


def _ffn_in(h, w_ffn_in, l):
    tm, tn = MM_TM, 512
    nb = D_FF // tn
    return pl.pallas_call(
        _ffn_in_kernel,
        grid=(nb, SEQ // tm),
        in_specs=[
            pl.BlockSpec((tm, D_MODEL), lambda j, m: (m, 0)),
            pl.BlockSpec((None, D_MODEL, tn), lambda j, m: (l, 0, j)),
            pl.BlockSpec((None, D_MODEL, tn), lambda j, m: (l, 0, nb + j)),
        ],
        out_specs=pl.BlockSpec((tm, tn), lambda j, m: (m, j)),
        out_shape=jax.ShapeDtypeStruct((SEQ, D_FF), BF16),
        scratch_shapes=[pltpu.VMEM((D_MODEL, tn), BF16), pltpu.VMEM((D_MODEL, tn), BF16)],
        compiler_params=_params("arbitrary", "arbitrary"),
        name="ffn_in",
    )(h, w_ffn_in, w_ffn_in)


def _sb_kernel(q_ref, k_ref, v_ref, u_ref, o_ref, z_ref, acc_ref, carry_ref):
    t = SB_T
    i = pl.program_id(1)
    u = u_ref[...]
    units = [(s, g) for s in range(SB_QS) for g in range(SB_G)]

    def q_rows(s):
        return slice(s * t, (s + 1) * t)

    def cols(g):
        return slice(g * HEAD_DIM, (g + 1) * HEAD_DIM)

    def logits_into(kb, slot, subs):
        ks = pl.multiple_of(kb * t, t)
        for n, (s, g) in enumerate(units):
            if s in subs:
                z_ref[slot, n] = _dot_nt(q_ref[q_rows(s), cols(g)], k_ref[pl.ds(ks, t), cols(g)])

    def tile(kb, slot, kinds):
        ks = pl.multiple_of(kb * t, t)
        if 'diag' in kinds:
            row = lax.broadcasted_iota(jnp.int32, (t, t), 0)
            col = lax.broadcasted_iota(jnp.int32, (t, t), 1)
            causal = col < row
        for n, (s, g) in enumerate(units):
            if kinds[s] is None:
                continue
            diag = kinds[s] == 'diag'
            z = z_ref[slot, n]
            ls = jnp.minimum(z, 0.0) - jnp.log(1.0 + jnp.exp2(-jnp.abs(z))) * LOG2E
            lr = ls - z
            if diag:
                lr = jnp.where(causal, lr, 0.0)
            carry = carry_ref[n]
            after = _dot(lr.astype(BF16), u) + jnp.concatenate([carry] * (t // LANES), axis=1)
            w = jnp.exp2(ls + after)
            if diag:
                w = jnp.where(causal, w, 0.0)
            acc_ref[n] += _dot(w.astype(BF16), v_ref[pl.ds(ks, t), cols(g)])
            carry_ref[n] = carry + jnp.sum(lr, axis=-1, keepdims=True)

    acc_ref[...] = jnp.zeros_like(acc_ref)
    carry_ref[...] = jnp.zeros_like(carry_ref)
    all_subs = tuple(range(SB_QS))
    top = SB_QS * i + SB_QS - 1
    logits_into(top, 0, (SB_QS - 1,))
    for d in range(SB_QS):
        kinds = [None if s < SB_QS - 1 - d else ('diag' if s == SB_QS - 1 - d else 'full') for s in all_subs]
        tile(top - d, d % 2, kinds)
        nxt = tuple(s for s in all_subs if s >= SB_QS - 2 - d) if d < SB_QS - 1 else all_subs
        logits_into(jnp.maximum(top - d - 1, 0), (d + 1) % 2, nxt)

    assert SB_QS % 2 == 0

    def body(j, _):
        kb = SB_QS * i - 1 - 2 * j
        for half in range(2):
            tile(kb - half, half, ['full'] * SB_QS)
            logits_into(jnp.maximum(kb - half - 1, 0), 1 - half, all_subs)
        return 0

    lax.fori_loop(0, (SB_QS // 2) * i, body, 0)
    for n, (s, g) in enumerate(units):
        o_ref[q_rows(s), cols(g)] = acc_ref[n].astype(BF16)


def _sb_attention(p, u_tri):
    t = SB_T
    tq = SB_QS * t
    w = SB_G * HEAD_DIM
    ng = SB_HEADS // SB_G
    nu = SB_QS * SB_G
    return pl.pallas_call(
        _sb_kernel,
        grid=(ng, SEQ // tq),
        in_specs=[
            pl.BlockSpec((tq, w), lambda h, i: (i, h)),
            pl.BlockSpec((SEQ, w), lambda h, i: (0, ng + h)),
            pl.BlockSpec((SEQ, w), lambda h, i: (0, 2 * ng + h)),
            pl.BlockSpec((t, t), lambda h, i: (0, 0)),
        ],
        out_specs=pl.BlockSpec((tq, w), lambda h, i: (i, h)),
        out_shape=jax.ShapeDtypeStruct((SEQ, SB_W), BF16),
        scratch_shapes=[pltpu.VMEM((2, nu, t, t), F32), pltpu.VMEM((nu, t, HEAD_DIM), F32),
                        pltpu.VMEM((nu, t, LANES), F32)],
        compiler_params=_params("arbitrary", "arbitrary"),
        name="sb_attention",
    )(p, p, p, u_tri)


DIL_ROWS = 2048


def _stream_rows(start, dilation):
    if dilation == 1:
        return pl.ds(start, DIL_BLK)
    return pl.ds(start, DIL_BLK, stride=dilation)


def _dil_kernel(*refs, dilation, slopes):
    nh = DIL_HEADS_PER_GROUP
    ins = [refs[5 * hh:5 * hh + 5] for hh in range(nh)]
    o_refs = refs[5 * nh:6 * nh]
    l_refs = refs[6 * nh:7 * nh]
    n = pl.program_id(0)
    blk, r = DIL_BLK, dilation
    i_idx = lax.broadcasted_iota(jnp.int32, (blk, 2 * blk), 0)
    j_idx = lax.broadcasted_iota(jnp.int32, (blk, 2 * blk), 1)
    delta = blk + i_idx - j_idx
    in_window = (delta >= 0) & (delta <= blk)
    first_valid = in_window & (j_idx >= jnp.where(n > 0, 0, blk))
    delta_f = delta.astype(F32)
    biases = [(-slopes[hh] * r) * delta_f for hh in range(DIL_HEADS_PER_GROUP)]
    for rho in range(r):
        for b in range(DIL_ROWS // (r * blk)):
            cur = _stream_rows(rho + r * b * blk, r)
            prev = _stream_rows(rho + r * max(b - 1, 0) * blk, r)
            valid = first_valid if b == 0 else in_window
            for hh in range(nh):
                q_ref, kc_ref, kp_ref, vc_ref, vp_ref = ins[hh]
                k_prev = kp_ref[prev, :] if b == 0 else kc_ref[prev, :]
                v_prev = vp_ref[prev, :] if b == 0 else vc_ref[prev, :]
                kk = jnp.concatenate([k_prev, kc_ref[cur, :]], axis=0).astype(BF16)
                vv = jnp.concatenate([v_prev, vc_ref[cur, :]], axis=0).astype(BF16)
                s = _dot_nt(q_ref[cur, :].astype(BF16), kk)
                logits = jnp.where(valid, s + biases[hh], -jnp.inf)
                m = jnp.max(logits, axis=-1, keepdims=True)
                pexp = jnp.exp(logits - m)
                den = jnp.sum(pexp, axis=-1, keepdims=True)
                o_refs[hh][cur, :] = _dot(pexp.astype(BF16), vv) / den
                l_refs[hh][cur, :] = jnp.broadcast_to(m + jnp.log(den), (blk, HEAD_DIM))


def _dil_group(p_dil, g):
    _, r = DIL_GROUPS[g]
    prev_rows = r * DIL_BLK
    ratio = DIL_ROWS // prev_rows
    nh = DIL_HEADS_PER_GROUP
    slopes = tuple(2.0 ** (-8.0 * (g * nh + hh + 1) / DIL_HEADS) for hh in range(nh))
    cur = lambda part, hh: pl.BlockSpec((DIL_ROWS, HEAD_DIM), lambda n: (n, part * DIL_HEADS + g * nh + hh))
    prev = lambda part, hh: pl.BlockSpec((prev_rows, HEAD_DIM),
                                         lambda n: (jnp.maximum(n * ratio - 1, 0), part * DIL_HEADS + g * nh + hh))
    in_specs = []
    for hh in range(nh):
        in_specs += [cur(0, hh), cur(1, hh), prev(1, hh), cur(2, hh), prev(2, hh)]
    out_spec = pl.BlockSpec((DIL_ROWS, HEAD_DIM), lambda n: (n, 0))
    res = pl.pallas_call(
        functools.partial(_dil_kernel, dilation=r, slopes=slopes),
        grid=(SEQ // DIL_ROWS,),
        in_specs=in_specs,
        out_specs=[out_spec] * (2 * nh),
        out_shape=[jax.ShapeDtypeStruct((SEQ, HEAD_DIM), F32)] * (2 * nh),
        compiler_params=_params("arbitrary"),
        name=f"dil_group{g}",
    )(*([p_dil] * (5 * nh)))
    return res[:nh], res[nh:]


def _dil_mix_kernel(*refs):
    ng, nh = len(DIL_GROUPS), DIL_HEADS_PER_GROUP
    out_ref = refs[-1]
    for hh in range(nh):
        o = [refs[g * nh + hh][...] for g in range(ng)]
        lse = [refs[ng * nh + g * nh + hh][...] for g in range(ng)]
        m = functools.reduce(jnp.maximum, lse)
        e = [jnp.exp(x - m) for x in lse]
        num = functools.reduce(lambda a, b: a + b, [ei * oi for ei, oi in zip(e, o)])
        den = functools.reduce(lambda a, b: a + b, e)
        out_ref[:, hh * HEAD_DIM:(hh + 1) * HEAD_DIM] = (num / den).astype(BF16)


def _dil_mix(outs, lses):
    tm = 1024
    flat = [a for grp in outs for a in grp] + [a for grp in lses for a in grp]
    return pl.pallas_call(
        _dil_mix_kernel,
        grid=(SEQ // tm,),
        in_specs=[pl.BlockSpec((tm, HEAD_DIM), lambda i: (i, 0))] * len(flat),
        out_specs=pl.BlockSpec((tm, DIL_OUT_W), lambda i: (i, 0)),
        out_shape=jax.ShapeDtypeStruct((SEQ, DIL_OUT_W), BF16),
        compiler_params=_params("arbitrary"),
        name="dil_mix",
    )(*flat)


def _gla_sum_matrix():
    L = GLA_L
    t = np.arange(L)[:, None]
    s = np.arange(L)[None, :]
    mats = []
    for lev in range(GLA_LEVELS):
        size = 2 << lev
        half = size // 2
        mid = (t // size) * size + half
        later = (t % size) >= half
        mats.append(np.where(later, (s >= mid) & (s <= t), (s > t) & (s < mid)))
    mats.append(s <= t)
    mats.append(s > t)
    return np.concatenate(mats, axis=0).astype(np.float32)


def _gla_kernel(q_ref, k_ref, v0_ref, v1_ref, r0_ref, r1_ref, a_ref, wa_ref, ba_ref, gain_ref, mst_ref,
                o_ref, st_ref):
    L = GLA_L

    @pl.when(pl.program_id(0) == 0)
    def _():
        st_ref[...] = jnp.zeros_like(st_ref)

    row = lax.broadcasted_iota(jnp.int32, (L, L), 0)
    col = lax.broadcasted_iota(jnp.int32, (L, L), 1)
    mst = mst_ref[...]
    gain = gain_ref[...]
    for blk, h in [(b, hh) for b in range(GLA_NB) for hh in range(GLA_HEADS)]:
        rs = slice(blk * L, (blk + 1) * L)
        ks = slice(h * GLA_DK, (h + 1) * GLA_DK)
        a_hi, a_lo = _split2(a_ref[rs, 0:GLA_GATE_RANK])
        wa_hi, wa_lo = _split2(wa_ref[:, ks])
        a = _dot(a_hi, wa_hi) + _dot(a_lo, wa_hi) + _dot(a_hi, wa_lo) + ba_ref[:, ks]
        log_a = _log_sigmoid(a) * (1.0 / GLA_GATE_TAU)
        p1, p2 = _split2(log_a)
        e_all = _dot(mst, p1) + _dot(mst, p2)
        qb = q_ref[rs, ks]
        kb = k_ref[rs, ks]
        q = qb.astype(F32)
        k = kb.astype(F32)
        scores = jnp.where(row == col, _dot_nt(qb, kb), 0.0)
        for lev in range(GLA_LEVELS):
            size = 2 << lev
            e = jnp.exp(e_all[lev * L:(lev + 1) * L])
            later = (row & (size // 2)) != 0
            qt = jnp.where(later, q * e, 0.0).astype(BF16)
            kt = jnp.where(later, 0.0, k * e).astype(BF16)
            part = _dot_nt(qt, kt)
            if size < L:
                part = jnp.where((row >> (lev + 1)) == (col >> (lev + 1)), part, 0.0)
            scores = scores + part
        e_cum = e_all[GLA_LEVELS * L:(GLA_LEVELS + 1) * L]
        e_rev = e_all[(GLA_LEVELS + 1) * L:(GLA_LEVELS + 2) * L]
        total = e_cum[L - 1:L]
        q_in = (q * jnp.exp(e_cum)).astype(BF16)
        k_out = (k * jnp.exp(e_rev)).astype(BF16)
        v_src = v0_ref if h < 2 else v1_ref
        r_src = r0_ref if h < 2 else r1_ref
        vs = slice((h % 2) * GLA_DV, (h % 2 + 1) * GLA_DV)
        v = v_src[rs, vs]
        st = st_ref[h]
        o = _dot_nt(q_in, st.astype(BF16)) + _dot(scores.astype(BF16), v)
        o = o * (GLA_DK ** -0.5)
        v_t = v.astype(F32).T.astype(BF16)
        st_ref[h] = st * jnp.exp(total) + _dot(v_t, k_out)
        ms = jnp.mean(o * o, axis=-1, keepdims=True)
        y = o * lax.rsqrt(ms + EPS) * gain
        r = r_src[rs, vs].astype(F32)
        o_ref[rs, h * GLA_DV:(h + 1) * GLA_DV] = (y * (r * jax.nn.sigmoid(r))).astype(BF16)


def _gla(p, a_cols, w_gla_a, b_gla_a, gla_out_gain, mst, l):
    L = GLA_L
    rows = GLA_NB * L
    w = 512
    col = lambda c: pl.BlockSpec((rows, w), lambda i: (i, c))
    return pl.pallas_call(
        _gla_kernel,
        grid=(SEQ // rows,),
        in_specs=[col(0), col(1), col(2), col(3), col(4), col(5),
                  pl.BlockSpec((rows, LANES), lambda i: (i, 0)),
                  pl.BlockSpec((None, GLA_GATE_RANK, GLA_QK_W), lambda i: (l, 0, 0)),
                  pl.BlockSpec((None, 1, GLA_QK_W), lambda i: (l, 0, 0)),
                  pl.BlockSpec((None, 1, GLA_DV), lambda i: (l, 0, 0)),
                  pl.BlockSpec(((GLA_LEVELS + 2) * L, L), lambda i: (0, 0))],
        out_specs=pl.BlockSpec((rows, GLA_V_W), lambda i: (i, 0)),
        out_shape=jax.ShapeDtypeStruct((SEQ, GLA_V_W), BF16),
        scratch_shapes=[pltpu.VMEM((GLA_HEADS, GLA_DV, GLA_DK), F32)],
        compiler_params=_params("arbitrary"),
        name="gla",
    )(p, p, p, p, p, p, a_cols, w_gla_a, b_gla_a.reshape(DEPTH, 1, GLA_QK_W),
      gla_out_gain.reshape(DEPTH, 1, GLA_DV), mst)


def _qk_gains(q_gain, k_gain, heads, q_scale):
    ones = jnp.ones((heads * HEAD_DIM,), F32)
    return jnp.concatenate([jnp.tile(q_gain * q_scale, heads), jnp.tile(k_gain, heads), ones]).reshape(1, -1)


def kernel(x, c, w_ada, b_ada, norm1_gain, norm2_gain, w_in, sb_q_gain, sb_k_gain, dil_q_gain, dil_k_gain,
           w_gla_a, b_gla_a, gla_out_gain, w_br_sb, w_br_dil, w_br_gla, w_out, w_ffn_in, w_ffn_out):
    assert x.shape == (1, SEQ, D_MODEL) and c.shape == (1, D_MODEL)
    xs = x.reshape(SEQ, D_MODEL)
    mod = _ada(c, w_ada, b_ada)
    u_tri = jnp.asarray(np.tril(np.ones((SB_T, SB_T), np.float32), -1), BF16)
    mst = jnp.asarray(_gla_sum_matrix(), BF16)
    scale = HEAD_DIM ** -0.5
    w_in_t = jnp.swapaxes(w_in, 1, 2)
    for l in range(DEPTH):
        h = _norm_mod(xs, norm1_gain, mod, l, 0, 1)
        p_sb = _inproj_qkv(h, w_in_t, l, OFF_SB, _qk_gains(sb_q_gain[l], sb_k_gain[l], SB_HEADS, scale * LOG2E),
                           BF16, "inproj_sb")
        p_dil = _inproj_qkv(h, w_in_t, l, OFF_DIL, _qk_gains(dil_q_gain[l], dil_k_gain[l], DIL_HEADS, scale),
                            F32, "inproj_dil")
        p = _inproj_plain(h, w_in_t, l, OFF_GLA, SEG_GLA_W, IN_TN, BF16, "inproj_gla")
        a_cols = _inproj_plain(h, w_in_t, l, OFF_A, LANES, LANES, F32, "inproj_a")
        gates = _inproj_gates(h, w_in_t, l)
        o_sb = _sb_attention(p_sb, u_tri)
        dil = [_dil_group(p_dil, g) for g in range(len(DIL_GROUPS))]
        o_dil = _dil_mix([d[0] for d in dil], [d[1] for d in dil])
        o_gla = _gla(p, a_cols, w_gla_a, b_gla_a, gla_out_gain, mst, l)
        y = _branch(o_sb, o_dil, o_gla, w_br_sb, w_br_dil, w_br_gla, gates, l)
        xs = _resid_proj(y, w_out, xs, mod, l, 2, MM_TM, 512)
        h2 = _norm_mod(xs, norm2_gain, mod, l, 3, 4)
        u = _ffn_in(h2, w_ffn_in, l)
        xs = _resid_proj(u, w_ffn_out, xs, mod, l, 5, 512, 512)
    return xs.reshape(1, SEQ, D_MODEL)
```

```python
import functools
import math

import numpy as np
import jax
import jax.numpy as jnp
from jax import lax
from jax.experimental import pallas as pl
from jax.experimental.pallas import tpu as pltpu

F32 = jnp.float32
BF16 = jnp.bfloat16

D_MODEL = 2048
SEQ = 8192
DEPTH = 2
HEAD_DIM = 128
SB_HEADS = 6
DIL_GROUPS = ((128, 1), (512, 4), (2048, 16))
DIL_HEADS_PER_GROUP = 2
DIL_HEADS = DIL_HEADS_PER_GROUP * len(DIL_GROUPS)
GLA_HEADS = 4
GLA_DK = 128
GLA_DV = 256
GLA_GATE_RANK = 16
GLA_GATE_TAU = 16.0
N_BRANCH = 3
D_FF = 5632
N_MOD = 6
EPS = 1e-6

SB_W = SB_HEADS * HEAD_DIM
DIL_W = DIL_HEADS * HEAD_DIM
DIL_OUT_W = DIL_HEADS_PER_GROUP * HEAD_DIM
GLA_QK_W = GLA_HEADS * GLA_DK
GLA_V_W = GLA_HEADS * GLA_DV
OFF_SB = 0
OFF_DIL = 3 * SB_W
OFF_GLA = OFF_DIL + 3 * DIL_W
OFF_A = OFF_GLA + 2 * GLA_QK_W + 2 * GLA_V_W
OFF_GATE = OFF_A + GLA_GATE_RANK
SEG_SB_W = 3 * SB_W
SEG_DIL_W = 3 * DIL_W
SEG_GLA_W = 2 * GLA_QK_W + 2 * GLA_V_W

LANES = 128
VMEM_LIMIT = 56 * 1024 * 1024

MM_TM = 1024
IN_TN = 768
SB_T = 256
SB_G = 2
SB_QS = 2
LOG2E = 1.4426950408889634
SB_DEAD = -160.0
DIL_BLK = 128
GLA_L = 128
GLA_NB = 2
GLA_LEVELS = 7


def _params(*sem):
    return pltpu.CompilerParams(dimension_semantics=sem, vmem_limit_bytes=VMEM_LIMIT)


def _dot(a, b):
    return jnp.dot(a, b, preferred_element_type=F32)


def _dot_nt(a, b):
    return lax.dot_general(a, b, (((1,), (1,)), ((), ())), preferred_element_type=F32)


def _split2(x):
    hi = x.astype(BF16)
    lo = (x - hi.astype(F32)).astype(BF16)
    return hi, lo


def _split3(x):
    p1 = x.astype(BF16)
    r1 = x - p1.astype(F32)
    p2 = r1.astype(BF16)
    p3 = (r1 - p2.astype(F32)).astype(BF16)
    return p1, p2, p3


def _log_sigmoid(z):
    return jnp.minimum(z, 0.0) - jnp.log(1.0 + jnp.exp(-jnp.abs(z)))


def _ada_kernel(c_ref, w_ref, b_ref, o_ref):
    c = c_ref[...]
    s = jnp.broadcast_to(c * jax.nn.sigmoid(c), (8, D_MODEL))
    s_hi, s_lo = _split2(s)
    w_hi, w_lo = _split2(w_ref[...])
    acc = _dot(s_hi, w_hi) + _dot(s_lo, w_hi) + _dot(s_hi, w_lo)
    o_ref[...] = acc[0:1] + b_ref[...]


def _ada(c, w_ada, b_ada):
    tn = 1024
    n = N_MOD * D_MODEL
    return pl.pallas_call(
        _ada_kernel,
        grid=(DEPTH, n // tn),
        in_specs=[
            pl.BlockSpec((1, D_MODEL), lambda l, j: (0, 0)),
            pl.BlockSpec((None, D_MODEL, tn), lambda l, j: (l, 0, j)),
            pl.BlockSpec((None, 1, tn), lambda l, j: (l, 0, j)),
        ],
        out_specs=pl.BlockSpec((None, 1, tn), lambda l, j: (l, 0, j)),
        out_shape=jax.ShapeDtypeStruct((DEPTH, 1, n), F32),
        compiler_params=_params("arbitrary", "arbitrary"),
        name="ada_mod",
    )(c, w_ada, b_ada.reshape(DEPTH, 1, n))


def _norm_kernel(x_ref, g_ref, sc_ref, sh_ref, o_ref):
    x = x_ref[...]
    ms = jnp.mean(x * x, axis=-1, keepdims=True)
    y = x * lax.rsqrt(ms + EPS) * g_ref[...]
    o_ref[...] = (y * (1.0 + sc_ref[...]) + sh_ref[...]).astype(BF16)


def _norm_mod(x, gain, mod, l, shift_idx, scale_idx):
    tm = 512
    return pl.pallas_call(
        _norm_kernel,
        grid=(SEQ // tm,),
        in_specs=[
            pl.BlockSpec((tm, D_MODEL), lambda i: (i, 0)),
            pl.BlockSpec((None, 1, D_MODEL), lambda i: (l, 0, 0)),
            pl.BlockSpec((None, 1, D_MODEL), lambda i: (l, 0, scale_idx)),
            pl.BlockSpec((None, 1, D_MODEL), lambda i: (l, 0, shift_idx)),
        ],
        out_specs=pl.BlockSpec((tm, D_MODEL), lambda i: (i, 0)),
        out_shape=jax.ShapeDtypeStruct((SEQ, D_MODEL), BF16),
        compiler_params=_params("arbitrary"),
        name="norm_mod",
    )(x, gain.reshape(DEPTH, 1, D_MODEL), mod, mod)


def _cast_weights(pairs):
    @pl.when(pl.program_id(1) == 0)
    def _():
        for w_ref, wb_ref in pairs:
            wb_ref[...] = w_ref[...].astype(BF16)


XPOSE_CHUNK = 256


def _cast_weights_t(wt_ref, wb_ref, skip=0, tail_ref=None):
    @pl.when(pl.program_id(1) == 0)
    def _():
        for c in range(D_MODEL // XPOSE_CHUNK):
            cs = slice(c * XPOSE_CHUNK, (c + 1) * XPOSE_CHUNK)
            blk = wt_ref[skip:, cs]
            if tail_ref is not None:
                blk = jnp.concatenate([blk, tail_ref[:, cs]], axis=0)
            wb_ref[cs, :] = blk.T.astype(BF16)


def _inproj_qkv_kernel(a_ref, w_ref, g_ref, o_ref, wb_ref):
    _cast_weights_t(w_ref, wb_ref)
    acc = _dot(a_ref[...], wb_ref[...])
    is_norm = pl.program_id(0) < 2

    @pl.when(is_norm)
    def _():
        g = g_ref[...]
        for c in range(IN_TN // HEAD_DIM):
            cs = slice(c * HEAD_DIM, (c + 1) * HEAD_DIM)
            blk = acc[:, cs]
            ms = jnp.mean(blk * blk, axis=-1, keepdims=True)
            o_ref[:, cs] = (blk * lax.rsqrt(ms + EPS) * g[:, cs]).astype(o_ref.dtype)

    @pl.when(jnp.logical_not(is_norm))
    def _():
        o_ref[...] = acc.astype(o_ref.dtype)


def _inproj_qkv(h, w_in_t, l, col0, gains, out_dtype, name):
    tm, tn = MM_TM, IN_TN
    width = 3 * tn
    return pl.pallas_call(
        _inproj_qkv_kernel,
        grid=(width // tn, SEQ // tm),
        in_specs=[
            pl.BlockSpec((tm, D_MODEL), lambda j, m: (m, 0)),
            pl.BlockSpec((None, tn, D_MODEL), lambda j, m: (l, col0 // tn + j, 0)),
            pl.BlockSpec((1, tn), lambda j, m: (0, j)),
        ],
        out_specs=pl.BlockSpec((tm, tn), lambda j, m: (m, j)),
        out_shape=jax.ShapeDtypeStruct((SEQ, width), out_dtype),
        scratch_shapes=[pltpu.VMEM((D_MODEL, tn), BF16)],
        compiler_params=_params("arbitrary", "arbitrary"),
        name=name,
    )(h, w_in_t, gains)


def _plain_kernel(a_ref, w_ref, o_ref, wb_ref):
    _cast_weights_t(w_ref, wb_ref)
    o_ref[...] = _dot(a_ref[...], wb_ref[...]).astype(o_ref.dtype)


def _inproj_plain(h, w_in_t, l, col0, width, tn, out_dtype, name):
    tm = MM_TM
    return pl.pallas_call(
        _plain_kernel,
        grid=(width // tn, SEQ // tm),
        in_specs=[
            pl.BlockSpec((tm, D_MODEL), lambda j, m: (m, 0)),
            pl.BlockSpec((None, tn, D_MODEL), lambda j, m: (l, col0 // tn + j, 0)),
        ],
        out_specs=pl.BlockSpec((tm, tn), lambda j, m: (m, j)),
        out_shape=jax.ShapeDtypeStruct((SEQ, width), out_dtype),
        scratch_shapes=[pltpu.VMEM((D_MODEL, tn), BF16)],
        compiler_params=_params("arbitrary", "arbitrary"),
        name=name,
    )(h, w_in_t)


GATE_SHIFT = OFF_GATE - OFF_A


def _gates_kernel(a_ref, wm_ref, wx_ref, o_ref, wb_ref):
    _cast_weights_t(wm_ref, wb_ref, skip=GATE_SHIFT, tail_ref=wx_ref)
    o_ref[...] = jax.nn.sigmoid(_dot(a_ref[...], wb_ref[...])).astype(BF16)


def _inproj_gates(h, w_in_t, l):
    tm, tn = MM_TM, IN_TN
    n = N_BRANCH * D_MODEL
    return pl.pallas_call(
        _gates_kernel,
        grid=(n // tn, SEQ // tm),
        in_specs=[
            pl.BlockSpec((tm, D_MODEL), lambda j, m: (m, 0)),
            pl.BlockSpec((None, tn, D_MODEL), lambda j, m: (l, OFF_A // tn + j, 0)),
            pl.BlockSpec((None, GATE_SHIFT, D_MODEL), lambda j, m: (l, (OFF_A + (j + 1) * tn) // GATE_SHIFT, 0)),
        ],
        out_specs=pl.BlockSpec((tm, tn), lambda j, m: (m, j)),
        out_shape=jax.ShapeDtypeStruct((SEQ, n), BF16),
        scratch_shapes=[pltpu.VMEM((D_MODEL, tn), BF16)],
        compiler_params=_params("arbitrary", "arbitrary"),
        name="inproj_gates",
    )(h, w_in_t, w_in_t)


def _branch_kernel(a0, a1, a2, w0, w1, w2, g0, g1, g2, o_ref, wb0, wb1, wb2):
    _cast_weights([(w0, wb0), (w1, wb1), (w2, wb2)])
    y = g0[...].astype(F32) * _dot(a0[...], wb0[...])
    y = y + g1[...].astype(F32) * _dot(a1[...], wb1[...])
    y = y + g2[...].astype(F32) * _dot(a2[...], wb2[...])
    o_ref[...] = y.astype(BF16)


def _branch(o_sb, o_dil, o_gla, w_br_sb, w_br_dil, w_br_gla, gates, l):
    tm, tn = MM_TM, 512
    nb = D_MODEL // tn
    a_spec = lambda w: pl.BlockSpec((tm, w), lambda j, m: (m, 0))
    w_spec = lambda w: pl.BlockSpec((None, w, tn), lambda j, m: (l, 0, j))
    g_spec = lambda b: pl.BlockSpec((tm, tn), lambda j, m: (m, b * nb + j))
    return pl.pallas_call(
        _branch_kernel,
        grid=(nb, SEQ // tm),
        in_specs=[a_spec(SB_W), a_spec(DIL_OUT_W), a_spec(GLA_V_W),
                  w_spec(SB_W), w_spec(DIL_OUT_W), w_spec(GLA_V_W),
                  g_spec(0), g_spec(1), g_spec(2)],
        out_specs=pl.BlockSpec((tm, tn), lambda j, m: (m, j)),
        out_shape=jax.ShapeDtypeStruct((SEQ, D_MODEL), BF16),
        scratch_shapes=[pltpu.VMEM((SB_W, tn), BF16), pltpu.VMEM((DIL_OUT_W, tn), BF16),
                        pltpu.VMEM((GLA_V_W, tn), BF16)],
        compiler_params=_params("arbitrary", "arbitrary"),
        name="branch_proj",
    )(o_sb, o_dil, o_gla, w_br_sb, w_br_dil, w_br_gla, gates, gates, gates)


def _resid_kernel(a_ref, w_ref, x_ref, gate_ref, o_ref, wb_ref):
    _cast_weights([(w_ref, wb_ref)])
    o_ref[...] = x_ref[...] + gate_ref[...] * _dot(a_ref[...], wb_ref[...])


def _resid_proj(a, w, x, mod, l, gate_idx, tm, tn):
    k = a.shape[1]
    nb = D_MODEL // tn
    return pl.pallas_call(
        _resid_kernel,
        grid=(nb, SEQ // tm),
        in_specs=[
            pl.BlockSpec((tm, k), lambda j, m: (m, 0)),
            pl.BlockSpec((None, k, tn), lambda j, m: (l, 0, j)),
            pl.BlockSpec((tm, tn), lambda j, m: (m, j)),
            pl.BlockSpec((None, 1, tn), lambda j, m: (l, 0, gate_idx * nb + j)),
        ],
        out_specs=pl.BlockSpec((tm, tn), lambda j, m: (m, j)),
        out_shape=jax.ShapeDtypeStruct((SEQ, D_MODEL), F32),
        scratch_shapes=[pltpu.VMEM((k, tn), BF16)],
        compiler_params=_params("arbitrary", "arbitrary"),
        name="resid_proj",
    )(a, w, x, mod)


def _ffn_in_kernel(a_ref, wg_ref, wu_ref, o_ref, wgb_ref, wub_ref):
    _cast_weights([(wg_ref, wgb_ref), (wu_ref, wub_ref)])
    a = a_ref[...]
    g = _dot(a, wgb_ref[...])
    u = _dot(a, wub_ref[...])
    o_ref[...] = (g * jax.nn.sigmoid(g) * u).astype(BF16)


def _ffn_in(h, w_ffn_in, l):
    tm, tn = MM_TM, 512
    nb = D_FF // tn
    return pl.pallas_call(
        _ffn_in_kernel,
        grid=(nb, SEQ // tm),
        in_specs=[
            pl.BlockSpec((tm, D_MODEL), lambda j, m: (m, 0)),
            pl.BlockSpec((None, D_MODEL, tn), lambda j, m: (l, 0, j)),
            pl.BlockSpec((None, D_MODEL, tn), lambda j, m: (l, 0, nb + j)),
        ],
        out_specs=pl.BlockSpec((tm, tn), lambda j, m: (m, j)),
        out_shape=jax.ShapeDtypeStruct((SEQ, D_FF), BF16),
        scratch_shapes=[pltpu.VMEM((D_MODEL, tn), BF16), pltpu.VMEM((D_MODEL, tn), BF16)],
        compiler_params=_params("arbitrary", "arbitrary"),
        name="ffn_in",
    )(h, w_ffn_in, w_ffn_in)


def _sb_kernel(q_ref, k_ref, v_ref, u_ref, o_ref, z_ref, acc_ref, carry_ref):
    t = SB_T
    i = pl.program_id(1)
    u = u_ref[...]
    units = [(s, g) for s in range(SB_QS) for g in range(SB_G)]

    def q_rows(s):
        return slice(s * t, (s + 1) * t)

    def cols(g):
        return slice(g * HEAD_DIM, (g + 1) * HEAD_DIM)

    def logits_into(kb, slot, subs):
        ks = pl.multiple_of(kb * t, t)
        for n, (s, g) in enumerate(units):
            if s in subs:
                z_ref[slot, n] = _dot_nt(q_ref[q_rows(s), cols(g)], k_ref[pl.ds(ks, t), cols(g)])

    def tile(kb, slot, kinds):
        ks = pl.multiple_of(kb * t, t)
        if 'diag' in kinds:
            row = lax.broadcasted_iota(jnp.int32, (t, t), 0)
            col = lax.broadcasted_iota(jnp.int32, (t, t), 1)
            causal = col < row
        for n, (s, g) in enumerate(units):
            if kinds[s] is None:
                continue
            diag = kinds[s] == 'diag'
            z = z_ref[slot, n]
            ls = jnp.minimum(z, 0.0) - jnp.log(1.0 + jnp.exp2(-jnp.abs(z))) * LOG2E
            lr = ls - z
            if diag:
                lr = jnp.where(causal, lr, 0.0)
            carry = carry_ref[n]
            after = _dot(lr.astype(BF16), u) + jnp.concatenate([carry] * (t // LANES), axis=1)
            w = jnp.exp2(ls + after)
            if diag:
                w = jnp.where(causal, w, 0.0)
            acc_ref[n] += _dot(w.astype(BF16), v_ref[pl.ds(ks, t), cols(g)])
            carry_ref[n] = carry + jnp.sum(lr, axis=-1, keepdims=True)

    acc_ref[...] = jnp.zeros_like(acc_ref)
    carry_ref[...] = jnp.zeros_like(carry_ref)
    all_subs = tuple(range(SB_QS))
    top = SB_QS * i + SB_QS - 1
    logits_into(top, 0, (SB_QS - 1,))
    for d in range(SB_QS):
        kinds = [None if s < SB_QS - 1 - d else ('diag' if s == SB_QS - 1 - d else 'full') for s in all_subs]
        tile(top - d, d % 2, kinds)
        nxt = tuple(s for s in all_subs if s >= SB_QS - 2 - d) if d < SB_QS - 1 else all_subs
        logits_into(jnp.maximum(top - d - 1, 0), (d + 1) % 2, nxt)

    def top_carry():
        return jnp.max(carry_ref[...])

    def cond(c):
        j, top = c
        return jnp.logical_and(j < SB_QS * i, top > SB_DEAD)

    def body(c):
        j, _ = c
        kb = SB_QS * i - 1 - j
        tile(kb, (j + SB_QS) % 2, ['full'] * SB_QS)
        logits_into(jnp.maximum(kb - 1, 0), (j + SB_QS + 1) % 2, all_subs)
        return j + 1, top_carry()

    lax.while_loop(cond, body, (jnp.int32(0), top_carry()))
    for n, (s, g) in enumerate(units):
        o_ref[q_rows(s), cols(g)] = acc_ref[n].astype(BF16)


def _sb_attention(p, u_tri):
    t = SB_T
    tq = SB_QS * t
    w = SB_G * HEAD_DIM
    ng = SB_HEADS // SB_G
    nu = SB_QS * SB_G
    return pl.pallas_call(
        _sb_kernel,
        grid=(ng, SEQ // tq),
        in_specs=[
            pl.BlockSpec((tq, w), lambda h, i: (i, h)),
            pl.BlockSpec((SEQ, w), lambda h, i: (0, ng + h)),
            pl.BlockSpec((SEQ, w), lambda h, i: (0, 2 * ng + h)),
            pl.BlockSpec((t, t), lambda h, i: (0, 0)),
        ],
        out_specs=pl.BlockSpec((tq, w), lambda h, i: (i, h)),
        out_shape=jax.ShapeDtypeStruct((SEQ, SB_W), BF16),
        scratch_shapes=[pltpu.VMEM((2, nu, t, t), F32), pltpu.VMEM((nu, t, HEAD_DIM), F32),
                        pltpu.VMEM((nu, t, LANES), F32)],
        compiler_params=_params("arbitrary", "arbitrary"),
        name="sb_attention",
    )(p, p, p, u_tri)


DIL_ROWS = 2048


def _stream_rows(start, dilation):
    if dilation == 1:
        return pl.ds(start, DIL_BLK)
    return pl.ds(start, DIL_BLK, stride=dilation)


def _dil_kernel(*refs, dilation, slopes):
    nh = DIL_HEADS_PER_GROUP
    ins = [refs[5 * hh:5 * hh + 5] for hh in range(nh)]
    o_refs = refs[5 * nh:6 * nh]
    l_refs = refs[6 * nh:7 * nh]
    n = pl.program_id(0)
    blk, r = DIL_BLK, dilation
    i_idx = lax.broadcasted_iota(jnp.int32, (blk, 2 * blk), 0)
    j_idx = lax.broadcasted_iota(jnp.int32, (blk, 2 * blk), 1)
    delta = blk + i_idx - j_idx
    in_window = (delta >= 0) & (delta <= blk)
    first_valid = in_window & (j_idx >= jnp.where(n > 0, 0, blk))
    delta_f = delta.astype(F32)
    biases = [(-slopes[hh] * r) * delta_f for hh in range(DIL_HEADS_PER_GROUP)]
    for rho in range(r):
        for b in range(DIL_ROWS // (r * blk)):
            cur = _stream_rows(rho + r * b * blk, r)
            prev = _stream_rows(rho + r * max(b - 1, 0) * blk, r)
            valid = first_valid if b == 0 else in_window
            for hh in range(nh):
                q_ref, kc_ref, kp_ref, vc_ref, vp_ref = ins[hh]
                k_prev = kp_ref[prev, :] if b == 0 else kc_ref[prev, :]
                v_prev = vp_ref[prev, :] if b == 0 else vc_ref[prev, :]
                kk = jnp.concatenate([k_prev, kc_ref[cur, :]], axis=0).astype(BF16)
                vv = jnp.concatenate([v_prev, vc_ref[cur, :]], axis=0).astype(BF16)
                s = _dot_nt(q_ref[cur, :].astype(BF16), kk)
                logits = jnp.where(valid, s + biases[hh], -jnp.inf)
                m = jnp.max(logits, axis=-1, keepdims=True)
                pexp = jnp.exp(logits - m)
                den = jnp.sum(pexp, axis=-1, keepdims=True)
                o_refs[hh][cur, :] = _dot(pexp.astype(BF16), vv) / den
                l_refs[hh][cur, :] = jnp.broadcast_to(m + jnp.log(den), (blk, HEAD_DIM))


def _dil_group(p_dil, g):
    _, r = DIL_GROUPS[g]
    prev_rows = r * DIL_BLK
    ratio = DIL_ROWS // prev_rows
    nh = DIL_HEADS_PER_GROUP
    slopes = tuple(2.0 ** (-8.0 * (g * nh + hh + 1) / DIL_HEADS) for hh in range(nh))
    cur = lambda part, hh: pl.BlockSpec((DIL_ROWS, HEAD_DIM), lambda n: (n, part * DIL_HEADS + g * nh + hh))
    prev = lambda part, hh: pl.BlockSpec((prev_rows, HEAD_DIM),
                                         lambda n: (jnp.maximum(n * ratio - 1, 0), part * DIL_HEADS + g * nh + hh))
    in_specs = []
    for hh in range(nh):
        in_specs += [cur(0, hh), cur(1, hh), prev(1, hh), cur(2, hh), prev(2, hh)]
    out_spec = pl.BlockSpec((DIL_ROWS, HEAD_DIM), lambda n: (n, 0))
    res = pl.pallas_call(
        functools.partial(_dil_kernel, dilation=r, slopes=slopes),
        grid=(SEQ // DIL_ROWS,),
        in_specs=in_specs,
        out_specs=[out_spec] * (2 * nh),
        out_shape=[jax.ShapeDtypeStruct((SEQ, HEAD_DIM), F32)] * (2 * nh),
        compiler_params=_params("arbitrary"),
        name=f"dil_group{g}",
    )(*([p_dil] * (5 * nh)))
    return res[:nh], res[nh:]


def _dil_mix_kernel(*refs):
    ng, nh = len(DIL_GROUPS), DIL_HEADS_PER_GROUP
    out_ref = refs[-1]
    for hh in range(nh):
        o = [refs[g * nh + hh][...] for g in range(ng)]
        lse = [refs[ng * nh + g * nh + hh][...] for g in range(ng)]
        m = functools.reduce(jnp.maximum, lse)
        e = [jnp.exp(x - m) for x in lse]
        num = functools.reduce(lambda a, b: a + b, [ei * oi for ei, oi in zip(e, o)])
        den = functools.reduce(lambda a, b: a + b, e)
        out_ref[:, hh * HEAD_DIM:(hh + 1) * HEAD_DIM] = (num / den).astype(BF16)


def _dil_mix(outs, lses):
    tm = 1024
    flat = [a for grp in outs for a in grp] + [a for grp in lses for a in grp]
    return pl.pallas_call(
        _dil_mix_kernel,
        grid=(SEQ // tm,),
        in_specs=[pl.BlockSpec((tm, HEAD_DIM), lambda i: (i, 0))] * len(flat),
        out_specs=pl.BlockSpec((tm, DIL_OUT_W), lambda i: (i, 0)),
        out_shape=jax.ShapeDtypeStruct((SEQ, DIL_OUT_W), BF16),
        compiler_params=_params("arbitrary"),
        name="dil_mix",
    )(*flat)


def _gla_sum_matrix():
    L = GLA_L
    t = np.arange(L)[:, None]
    s = np.arange(L)[None, :]
    mats = []
    for lev in range(GLA_LEVELS):
        size = 2 << lev
        half = size // 2
        mid = (t // size) * size + half
        later = (t % size) >= half
        mats.append(np.where(later, (s >= mid) & (s <= t), (s > t) & (s < mid)))
    mats.append(s <= t)
    mats.append(s > t)
    return np.concatenate(mats, axis=0).astype(np.float32)


def _gla_kernel(q_ref, k_ref, v0_ref, v1_ref, r0_ref, r1_ref, a_ref, wa_ref, ba_ref, gain_ref, mst_ref,
                o_ref, st_ref):
    L = GLA_L

    @pl.when(pl.program_id(0) == 0)
    def _():
        st_ref[...] = jnp.zeros_like(st_ref)

    row = lax.broadcasted_iota(jnp.int32, (L, L), 0)
    col = lax.broadcasted_iota(jnp.int32, (L, L), 1)
    mst = mst_ref[...]
    gain = gain_ref[...]
    for blk, h in [(b, hh) for b in range(GLA_NB) for hh in range(GLA_HEADS)]:
        rs = slice(blk * L, (blk + 1) * L)
        ks = slice(h * GLA_DK, (h + 1) * GLA_DK)
        a_hi, a_lo = _split2(a_ref[rs, 0:GLA_GATE_RANK])
        wa_hi, wa_lo = _split2(wa_ref[:, ks])
        a = _dot(a_hi, wa_hi) + _dot(a_lo, wa_hi) + _dot(a_hi, wa_lo) + ba_ref[:, ks]
        log_a = _log_sigmoid(a) * (1.0 / GLA_GATE_TAU)
        p1, p2 = _split2(log_a)
        e_all = _dot(mst, p1) + _dot(mst, p2)
        qb = q_ref[rs, ks]
        kb = k_ref[rs, ks]
        q = qb.astype(F32)
        k = kb.astype(F32)
        scores = jnp.where(row == col, _dot_nt(qb, kb), 0.0)
        for lev in range(GLA_LEVELS):
            size = 2 << lev
            e = jnp.exp(e_all[lev * L:(lev + 1) * L])
            later = (row & (size // 2)) != 0
            qt = jnp.where(later, q * e, 0.0).astype(BF16)
            kt = jnp.where(later, 0.0, k * e).astype(BF16)
            part = _dot_nt(qt, kt)
            if size < L:
                part = jnp.where((row >> (lev + 1)) == (col >> (lev + 1)), part, 0.0)
            scores = scores + part
        e_cum = e_all[GLA_LEVELS * L:(GLA_LEVELS + 1) * L]
        e_rev = e_all[(GLA_LEVELS + 1) * L:(GLA_LEVELS + 2) * L]
        total = e_cum[L - 1:L]
        q_in = (q * jnp.exp(e_cum)).astype(BF16)
        k_out = (k * jnp.exp(e_rev)).astype(BF16)
        v_src = v0_ref if h < 2 else v1_ref
        r_src = r0_ref if h < 2 else r1_ref
        vs = slice((h % 2) * GLA_DV, (h % 2 + 1) * GLA_DV)
        v = v_src[rs, vs]
        st = st_ref[h]
        o = _dot_nt(q_in, st.astype(BF16)) + _dot(scores.astype(BF16), v)
        o = o * (GLA_DK ** -0.5)
        v_t = v.astype(F32).T.astype(BF16)
        st_ref[h] = st * jnp.exp(total) + _dot(v_t, k_out)
        ms = jnp.mean(o * o, axis=-1, keepdims=True)
        y = o * lax.rsqrt(ms + EPS) * gain
        r = r_src[rs, vs].astype(F32)
        o_ref[rs, h * GLA_DV:(h + 1) * GLA_DV] = (y * (r * jax.nn.sigmoid(r))).astype(BF16)


def _gla(p, a_cols, w_gla_a, b_gla_a, gla_out_gain, mst, l):
    L = GLA_L
    rows = GLA_NB * L
    w = 512
    col = lambda c: pl.BlockSpec((rows, w), lambda i: (i, c))
    return pl.pallas_call(
        _gla_kernel,
        grid=(SEQ // rows,),
        in_specs=[col(0), col(1), col(2), col(3), col(4), col(5),
                  pl.BlockSpec((rows, LANES), lambda i: (i, 0)),
                  pl.BlockSpec((None, GLA_GATE_RANK, GLA_QK_W), lambda i: (l, 0, 0)),
                  pl.BlockSpec((None, 1, GLA_QK_W), lambda i: (l, 0, 0)),
                  pl.BlockSpec((None, 1, GLA_DV), lambda i: (l, 0, 0)),
                  pl.BlockSpec(((GLA_LEVELS + 2) * L, L), lambda i: (0, 0))],
        out_specs=pl.BlockSpec((rows, GLA_V_W), lambda i: (i, 0)),
        out_shape=jax.ShapeDtypeStruct((SEQ, GLA_V_W), BF16),
        scratch_shapes=[pltpu.VMEM((GLA_HEADS, GLA_DV, GLA_DK), F32)],
        compiler_params=_params("arbitrary"),
        name="gla",
    )(p, p, p, p, p, p, a_cols, w_gla_a, b_gla_a.reshape(DEPTH, 1, GLA_QK_W),
      gla_out_gain.reshape(DEPTH, 1, GLA_DV), mst)


def _qk_gains(q_gain, k_gain, heads, q_scale):
    ones = jnp.ones((heads * HEAD_DIM,), F32)
    return jnp.concatenate([jnp.tile(q_gain * q_scale, heads), jnp.tile(k_gain, heads), ones]).reshape(1, -1)


def kernel(x, c, w_ada, b_ada, norm1_gain, norm2_gain, w_in, sb_q_gain, sb_k_gain, dil_q_gain, dil_k_gain,
           w_gla_a, b_gla_a, gla_out_gain, w_br_sb, w_br_dil, w_br_gla, w_out, w_ffn_in, w_ffn_out):
    assert x.shape == (1, SEQ, D_MODEL) and c.shape == (1, D_MODEL)
    xs = x.reshape(SEQ, D_MODEL)
    mod = _ada(c, w_ada, b_ada)
    u_tri = jnp.asarray(np.tril(np.ones((SB_T, SB_T), np.float32), -1), BF16)
    mst = jnp.asarray(_gla_sum_matrix(), BF16)
    scale = HEAD_DIM ** -0.5
    w_in_t = jnp.swapaxes(w_in, 1, 2)
    for l in range(DEPTH):
        h = _norm_mod(xs, norm1_gain, mod, l, 0, 1)
        p_sb = _inproj_qkv(h, w_in_t, l, OFF_SB, _qk_gains(sb_q_gain[l], sb_k_gain[l], SB_HEADS, scale * LOG2E),
                           BF16, "inproj_sb")
        p_dil = _inproj_qkv(h, w_in_t, l, OFF_DIL, _qk_gains(dil_q_gain[l], dil_k_gain[l], DIL_HEADS, scale),
                            F32, "inproj_dil")
        p = _inproj_plain(h, w_in_t, l, OFF_GLA, SEG_GLA_W, IN_TN, BF16, "inproj_gla")
        a_cols = _inproj_plain(h, w_in_t, l, OFF_A, LANES, LANES, F32, "inproj_a")
        gates = _inproj_gates(h, w_in_t, l)
        o_sb = _sb_attention(p_sb, u_tri)
        dil = [_dil_group(p_dil, g) for g in range(len(DIL_GROUPS))]
        o_dil = _dil_mix([d[0] for d in dil], [d[1] for d in dil])
        o_gla = _gla(p, a_cols, w_gla_a, b_gla_a, gla_out_gain, mst, l)
        y = _branch(o_sb, o_dil, o_gla, w_br_sb, w_br_dil, w_br_gla, gates, l)
        xs = _resid_proj(y, w_out, xs, mod, l, 2, MM_TM, 512)
        h2 = _norm_mod(xs, norm2_gain, mod, l, 3, 4)
        u = _ffn_in(h2, w_ffn_in, l)
        xs = _resid_proj(u, w_ffn_out, xs, mod, l, 5, 512, 512)
    return xs.reshape(1, SEQ, D_MODEL)
```

```python
import functools
import math

import numpy as np
import jax
import jax.numpy as jnp
from jax import lax
from jax.experimental import pallas as pl
from jax.experimental.pallas import tpu as pltpu

F32 = jnp.float32
BF16 = jnp.bfloat16

D_MODEL = 2048
SEQ = 8192
DEPTH = 2
HEAD_DIM = 128
SB_HEADS = 6
DIL_GROUPS = ((128, 1), (512, 4), (2048, 16))
DIL_HEADS_PER_GROUP = 2
DIL_HEADS = DIL_HEADS_PER_GROUP * len(DIL_GROUPS)
GLA_HEADS = 4
GLA_DK = 128
GLA_DV = 256
GLA_GATE_RANK = 16
GLA_GATE_TAU = 16.0
N_BRANCH = 3
D_FF = 5632
N_MOD = 6
EPS = 1e-6

SB_W = SB_HEADS * HEAD_DIM
DIL_W = DIL_HEADS * HEAD_DIM
DIL_OUT_W = DIL_HEADS_PER_GROUP * HEAD_DIM
GLA_QK_W = GLA_HEADS * GLA_DK
GLA_V_W = GLA_HEADS * GLA_DV
OFF_SB = 0
OFF_DIL = 3 * SB_W
OFF_GLA = OFF_DIL + 3 * DIL_W
OFF_A = OFF_GLA + 2 * GLA_QK_W + 2 * GLA_V_W
OFF_GATE = OFF_A + GLA_GATE_RANK
SEG_SB_W = 3 * SB_W
SEG_DIL_W = 3 * DIL_W
SEG_GLA_W = 2 * GLA_QK_W + 2 * GLA_V_W

LANES = 128
VMEM_LIMIT = 56 * 1024 * 1024

MM_TM = 1024
IN_TN = 768
SB_T = 256
SB_G = 2
SB_QS = 2
LOG2E = 1.4426950408889634
SB_DEAD = -160.0
DIL_BLK = 128
GLA_L = 128
GLA_NB = 2
GLA_LEVELS = 7


def _params(*sem):
    return pltpu.CompilerParams(dimension_semantics=sem, vmem_limit_bytes=VMEM_LIMIT)


def _dot(a, b):
    return jnp.dot(a, b, preferred_element_type=F32)


def _dot_nt(a, b):
    return lax.dot_general(a, b, (((1,), (1,)), ((), ())), preferred_element_type=F32)


def _split2(x):
    hi = x.astype(BF16)
    lo = (x - hi.astype(F32)).astype(BF16)
    return hi, lo


def _split3(x):
    p1 = x.astype(BF16)
    r1 = x - p1.astype(F32)
    p2 = r1.astype(BF16)
    p3 = (r1 - p2.astype(F32)).astype(BF16)
    return p1, p2, p3


def _log_sigmoid(z):
    return jnp.minimum(z, 0.0) - jnp.log(1.0 + jnp.exp(-jnp.abs(z)))


def _ada_kernel(c_ref, w_ref, b_ref, o_ref):
    c = c_ref[...]
    s = jnp.broadcast_to(c * jax.nn.sigmoid(c), (8, D_MODEL))
    s_hi, s_lo = _split2(s)
    w_hi, w_lo = _split2(w_ref[...])
    acc = _dot(s_hi, w_hi) + _dot(s_lo, w_hi) + _dot(s_hi, w_lo)
    o_ref[...] = acc[0:1] + b_ref[...]


def _ada(c, w_ada, b_ada):
    tn = 1024
    n = N_MOD * D_MODEL
    return pl.pallas_call(
        _ada_kernel,
        grid=(DEPTH, n // tn),
        in_specs=[
            pl.BlockSpec((1, D_MODEL), lambda l, j: (0, 0)),
            pl.BlockSpec((None, D_MODEL, tn), lambda l, j: (l, 0, j)),
            pl.BlockSpec((None, 1, tn), lambda l, j: (l, 0, j)),
        ],
        out_specs=pl.BlockSpec((None, 1, tn), lambda l, j: (l, 0, j)),
        out_shape=jax.ShapeDtypeStruct((DEPTH, 1, n), F32),
        compiler_params=_params("arbitrary", "arbitrary"),
        name="ada_mod",
    )(c, w_ada, b_ada.reshape(DEPTH, 1, n))


def _norm_kernel(x_ref, g_ref, sc_ref, sh_ref, o_ref):
    x = x_ref[...]
    ms = jnp.mean(x * x, axis=-1, keepdims=True)
    y = x * lax.rsqrt(ms + EPS) * g_ref[...]
    o_ref[...] = (y * (1.0 + sc_ref[...]) + sh_ref[...]).astype(BF16)


def _norm_mod(x, gain, mod, l, shift_idx, scale_idx):
    tm = 512
    return pl.pallas_call(
        _norm_kernel,
        grid=(SEQ // tm,),
        in_specs=[
            pl.BlockSpec((tm, D_MODEL), lambda i: (i, 0)),
            pl.BlockSpec((None, 1, D_MODEL), lambda i: (l, 0, 0)),
            pl.BlockSpec((None, 1, D_MODEL), lambda i: (l, 0, scale_idx)),
            pl.BlockSpec((None, 1, D_MODEL), lambda i: (l, 0, shift_idx)),
        ],
        out_specs=pl.BlockSpec((tm, D_MODEL), lambda i: (i, 0)),
        out_shape=jax.ShapeDtypeStruct((SEQ, D_MODEL), BF16),
        compiler_params=_params("arbitrary"),
        name="norm_mod",
    )(x, gain.reshape(DEPTH, 1, D_MODEL), mod, mod)


def _cast_weights(pairs):
    @pl.when(pl.program_id(1) == 0)
    def _():
        for w_ref, wb_ref in pairs:
            wb_ref[...] = w_ref[...].astype(BF16)


MXU_N = 256


def _col_chunks(tn):
    return [slice(c, min(c + MXU_N, tn)) for c in range(0, tn, MXU_N)]


XPOSE_CHUNK = 256


def _cast_weights_t(wt_ref, wb_ref, skip=0, tail_ref=None):
    @pl.when(pl.program_id(1) == 0)
    def _():
        for c in range(D_MODEL // XPOSE_CHUNK):
            cs = slice(c * XPOSE_CHUNK, (c + 1) * XPOSE_CHUNK)
            blk = wt_ref[skip:, cs]
            if tail_ref is not None:
                blk = jnp.concatenate([blk, tail_ref[:, cs]], axis=0)
            wb_ref[cs, :] = blk.T.astype(BF16)


def _inproj_qkv_kernel(a_ref, w_ref, g_ref, o_ref, wb_ref):
    _cast_weights_t(w_ref, wb_ref)
    is_norm = pl.program_id(0) < 2
    a = a_ref[...]

    @pl.when(is_norm)
    def _():
        g = g_ref[...]
        for ns in _col_chunks(IN_TN):
            acc = _dot(a, wb_ref[:, ns])
            for c in range(MXU_N // HEAD_DIM):
                cs = slice(ns.start + c * HEAD_DIM, ns.start + (c + 1) * HEAD_DIM)
                blk = acc[:, c * HEAD_DIM:(c + 1) * HEAD_DIM]
                ms = jnp.mean(blk * blk, axis=-1, keepdims=True)
                o_ref[:, cs] = (blk * lax.rsqrt(ms + EPS) * g[:, cs]).astype(o_ref.dtype)

    @pl.when(jnp.logical_not(is_norm))
    def _():
        for ns in _col_chunks(IN_TN):
            o_ref[:, ns] = _dot(a, wb_ref[:, ns]).astype(o_ref.dtype)


def _inproj_qkv(h, w_in_t, l, col0, gains, out_dtype, name):
    tm, tn = MM_TM, IN_TN
    width = 3 * tn
    return pl.pallas_call(
        _inproj_qkv_kernel,
        grid=(width // tn, SEQ // tm),
        in_specs=[
            pl.BlockSpec((tm, D_MODEL), lambda j, m: (m, 0)),
            pl.BlockSpec((None, tn, D_MODEL), lambda j, m: (l, col0 // tn + j, 0)),
            pl.BlockSpec((1, tn), lambda j, m: (0, j)),
        ],
        out_specs=pl.BlockSpec((tm, tn), lambda j, m: (m, j)),
        out_shape=jax.ShapeDtypeStruct((SEQ, width), out_dtype),
        scratch_shapes=[pltpu.VMEM((D_MODEL, tn), BF16)],
        compiler_params=_params("arbitrary", "arbitrary"),
        name=name,
    )(h, w_in_t, gains)


def _plain_kernel(a_ref, w_ref, o_ref, wb_ref):
    _cast_weights_t(w_ref, wb_ref)
    a = a_ref[...]
    for ns in _col_chunks(o_ref.shape[1]):
        o_ref[:, ns] = _dot(a, wb_ref[:, ns]).astype(o_ref.dtype)


def _inproj_plain(h, w_in_t, l, col0, width, tn, out_dtype, name):
    tm = MM_TM
    return pl.pallas_call(
        _plain_kernel,
        grid=(width // tn, SEQ // tm),
        in_specs=[
            pl.BlockSpec((tm, D_MODEL), lambda j, m: (m, 0)),
            pl.BlockSpec((None, tn, D_MODEL), lambda j, m: (l, col0 // tn + j, 0)),
        ],
        out_specs=pl.BlockSpec((tm, tn), lambda j, m: (m, j)),
        out_shape=jax.ShapeDtypeStruct((SEQ, width), out_dtype),
        scratch_shapes=[pltpu.VMEM((D_MODEL, tn), BF16)],
        compiler_params=_params("arbitrary", "arbitrary"),
        name=name,
    )(h, w_in_t)


GATE_SHIFT = OFF_GATE - OFF_A


def _gates_kernel(a_ref, wm_ref, wx_ref, o_ref, wb_ref):
    _cast_weights_t(wm_ref, wb_ref, skip=GATE_SHIFT, tail_ref=wx_ref)
    a = a_ref[...]
    for ns in _col_chunks(IN_TN):
        o_ref[:, ns] = jax.nn.sigmoid(_dot(a, wb_ref[:, ns])).astype(BF16)


def _inproj_gates(h, w_in_t, l):
    tm, tn = MM_TM, IN_TN
    n = N_BRANCH * D_MODEL
    return pl.pallas_call(
        _gates_kernel,
        grid=(n // tn, SEQ // tm),
        in_specs=[
            pl.BlockSpec((tm, D_MODEL), lambda j, m: (m, 0)),
            pl.BlockSpec((None, tn, D_MODEL), lambda j, m: (l, OFF_A // tn + j, 0)),
            pl.BlockSpec((None, GATE_SHIFT, D_MODEL), lambda j, m: (l, (OFF_A + (j + 1) * tn) // GATE_SHIFT, 0)),
        ],
        out_specs=pl.BlockSpec((tm, tn), lambda j, m: (m, j)),
        out_shape=jax.ShapeDtypeStruct((SEQ, n), BF16),
        scratch_shapes=[pltpu.VMEM((D_MODEL, tn), BF16)],
        compiler_params=_params("arbitrary", "arbitrary"),
        name="inproj_gates",
    )(h, w_in_t, w_in_t)


def _branch_kernel(a0, a1, a2, w0, w1, w2, g0, g1, g2, o_ref, wb0, wb1, wb2):
    _cast_weights([(w0, wb0), (w1, wb1), (w2, wb2)])
    x0, x1, x2 = a0[...], a1[...], a2[...]
    for ns in _col_chunks(o_ref.shape[1]):
        y = g0[:, ns].astype(F32) * _dot(x0, wb0[:, ns])
        y = y + g1[:, ns].astype(F32) * _dot(x1, wb1[:, ns])
        y = y + g2[:, ns].astype(F32) * _dot(x2, wb2[:, ns])
        o_ref[:, ns] = y.astype(BF16)


def _branch(o_sb, o_dil, o_gla, w_br_sb, w_br_dil, w_br_gla, gates, l):
    tm, tn = MM_TM, 1024
    nb = D_MODEL // tn
    a_spec = lambda w: pl.BlockSpec((tm, w), lambda j, m: (m, 0))
    w_spec = lambda w: pl.BlockSpec((None, w, tn), lambda j, m: (l, 0, j))
    g_spec = lambda b: pl.BlockSpec((tm, tn), lambda j, m: (m, b * nb + j))
    return pl.pallas_call(
        _branch_kernel,
        grid=(nb, SEQ // tm),
        in_specs=[a_spec(SB_W), a_spec(DIL_OUT_W), a_spec(GLA_V_W),
                  w_spec(SB_W), w_spec(DIL_OUT_W), w_spec(GLA_V_W),
                  g_spec(0), g_spec(1), g_spec(2)],
        out_specs=pl.BlockSpec((tm, tn), lambda j, m: (m, j)),
        out_shape=jax.ShapeDtypeStruct((SEQ, D_MODEL), BF16),
        scratch_shapes=[pltpu.VMEM((SB_W, tn), BF16), pltpu.VMEM((DIL_OUT_W, tn), BF16),
                        pltpu.VMEM((GLA_V_W, tn), BF16)],
        compiler_params=_params("arbitrary", "arbitrary"),
        name="branch_proj",
    )(o_sb, o_dil, o_gla, w_br_sb, w_br_dil, w_br_gla, gates, gates, gates)


def _resid_kernel(a_ref, w_ref, x_ref, gate_ref, o_ref, wb_ref):
    _cast_weights([(w_ref, wb_ref)])
    a = a_ref[...]
    for ns in _col_chunks(o_ref.shape[1]):
        o_ref[:, ns] = x_ref[:, ns] + gate_ref[:, ns] * _dot(a, wb_ref[:, ns])


def _resid_proj(a, w, x, mod, l, gate_idx, tm, tn):
    k = a.shape[1]
    nb = D_MODEL // tn
    return pl.pallas_call(
        _resid_kernel,
        grid=(nb, SEQ // tm),
        in_specs=[
            pl.BlockSpec((tm, k), lambda j, m: (m, 0)),
            pl.BlockSpec((None, k, tn), lambda j, m: (l, 0, j)),
            pl.BlockSpec((tm, tn), lambda j, m: (m, j)),
            pl.BlockSpec((None, 1, tn), lambda j, m: (l, 0, gate_idx * nb + j)),
        ],
        out_specs=pl.BlockSpec((tm, tn), lambda j, m: (m, j)),
        out_shape=jax.ShapeDtypeStruct((SEQ, D_MODEL), F32),
        scratch_shapes=[pltpu.VMEM((k, tn), BF16)],
        compiler_params=_params("arbitrary", "arbitrary"),
        name="resid_proj",
    )(a, w, x, mod)


def _resid_norm_kernel(a_ref, w_ref, x_ref, gate_ref, g_ref, sc_ref, sh_ref, o_ref, h_ref, wb_ref):
    @pl.when(pl.program_id(0) == 0)
    def _():
        wb_ref[...] = w_ref[...].astype(BF16)

    a = a_ref[...]
    for ns in _col_chunks(D_MODEL):
        o_ref[:, ns] = x_ref[:, ns] + gate_ref[:, ns] * _dot(a, wb_ref[:, ns])
    x = o_ref[...]
    ms = jnp.mean(x * x, axis=-1, keepdims=True)
    y = x * lax.rsqrt(ms + EPS) * g_ref[...]
    h_ref[...] = (y * (1.0 + sc_ref[...]) + sh_ref[...]).astype(BF16)


def _resid_proj_norm(a, w, x, mod, l, gate_idx, gain, shift_idx, scale_idx):
    tm = 512
    k = a.shape[1]
    row = lambda idx: pl.BlockSpec((None, 1, D_MODEL), lambda m: (l, 0, idx))
    return pl.pallas_call(
        _resid_norm_kernel,
        grid=(SEQ // tm,),
        in_specs=[
            pl.BlockSpec((tm, k), lambda m: (m, 0)),
            pl.BlockSpec((None, k, D_MODEL), lambda m: (l, 0, 0), pipeline_mode=pl.Buffered(1)),
            pl.BlockSpec((tm, D_MODEL), lambda m: (m, 0)),
            row(gate_idx), row(0), row(scale_idx), row(shift_idx),
        ],
        out_specs=[pl.BlockSpec((tm, D_MODEL), lambda m: (m, 0)), pl.BlockSpec((tm, D_MODEL), lambda m: (m, 0))],
        out_shape=[jax.ShapeDtypeStruct((SEQ, D_MODEL), F32), jax.ShapeDtypeStruct((SEQ, D_MODEL), BF16)],
        scratch_shapes=[pltpu.VMEM((k, D_MODEL), BF16)],
        compiler_params=_params("arbitrary"),
        name="resid_proj_norm",
    )(a, w, x, mod, gain.reshape(DEPTH, 1, D_MODEL), mod, mod)


def _ffn_in_kernel(a_ref, wg_ref, wu_ref, o_ref, wgb_ref, wub_ref):
    _cast_weights([(wg_ref, wgb_ref), (wu_ref, wub_ref)])
    a = a_ref[...]
    for ns in _col_chunks(o_ref.shape[1]):
        g = _dot(a, wgb_ref[:, ns])
        u = _dot(a, wub_ref[:, ns])
        o_ref[:, ns] = (g * jax.nn.sigmoid(g) * u).astype(BF16)


def _ffn_in(h, w_ffn_in, l):
    tm, tn = MM_TM, 512
    nb = D_FF // tn
    return pl.pallas_call(
        _ffn_in_kernel,
        grid=(nb, SEQ // tm),
        in_specs=[
            pl.BlockSpec((tm, D_MODEL), lambda j, m: (m, 0)),
            pl.BlockSpec((None, D_MODEL, tn), lambda j, m: (l, 0, j)),
            pl.BlockSpec((None, D_MODEL, tn), lambda j, m: (l, 0, nb + j)),
        ],
        out_specs=pl.BlockSpec((tm, tn), lambda j, m: (m, j)),
        out_shape=jax.ShapeDtypeStruct((SEQ, D_FF), BF16),
        scratch_shapes=[pltpu.VMEM((D_MODEL, tn), BF16), pltpu.VMEM((D_MODEL, tn), BF16)],
        compiler_params=_params("arbitrary", "arbitrary"),
        name="ffn_in",
    )(h, w_ffn_in, w_ffn_in)


def _sb_kernel(q_ref, k_ref, v_ref, u_ref, o_ref, z_ref, acc_ref, carry_ref):
    t = SB_T
    i = pl.program_id(1)
    u = u_ref[...]
    units = [(s, g) for s in range(SB_QS) for g in range(SB_G)]

    def q_rows(s):
        return slice(s * t, (s + 1) * t)

    def cols(g):
        return slice(g * HEAD_DIM, (g + 1) * HEAD_DIM)

    def logits_into(kb, slot, subs):
        ks = pl.multiple_of(kb * t, t)
        for n, (s, g) in enumerate(units):
            if s in subs:
                z_ref[slot, n] = _dot_nt(q_ref[q_rows(s), cols(g)], k_ref[pl.ds(ks, t), cols(g)])

    def tile(kb, slot, kinds):
        ks = pl.multiple_of(kb * t, t)
        if 'diag' in kinds:
            row = lax.broadcasted_iota(jnp.int32, (t, t), 0)
            col = lax.broadcasted_iota(jnp.int32, (t, t), 1)
            causal = col < row
        for n, (s, g) in enumerate(units):
            if kinds[s] is None:
                continue
            diag = kinds[s] == 'diag'
            z = z_ref[slot, n]
            ls = jnp.minimum(z, 0.0) - jnp.log(1.0 + jnp.exp2(-jnp.abs(z))) * LOG2E
            lr = ls - z
            if diag:
                lr = jnp.where(causal, lr, 0.0)
            carry = carry_ref[n]
            after = _dot(lr.astype(BF16), u) + jnp.concatenate([carry] * (t // LANES), axis=1)
            w = jnp.exp2(ls + after)
            if diag:
                w = jnp.where(causal, w, 0.0)
            acc_ref[n] += _dot(w.astype(BF16), v_ref[pl.ds(ks, t), cols(g)])
            carry_ref[n] = carry + jnp.sum(lr, axis=-1, keepdims=True)

    acc_ref[...] = jnp.zeros_like(acc_ref)
    carry_ref[...] = jnp.zeros_like(carry_ref)
    all_subs = tuple(range(SB_QS))
    top = SB_QS * i + SB_QS - 1
    logits_into(top, 0, (SB_QS - 1,))
    for d in range(SB_QS):
        kinds = [None if s < SB_QS - 1 - d else ('diag' if s == SB_QS - 1 - d else 'full') for s in all_subs]
        tile(top - d, d % 2, kinds)
        nxt = tuple(s for s in all_subs if s >= SB_QS - 2 - d) if d < SB_QS - 1 else all_subs
        logits_into(jnp.maximum(top - d - 1, 0), (d + 1) % 2, nxt)

    def top_carry():
        return jnp.max(carry_ref[...])

    def cond(c):
        j, top = c
        return jnp.logical_and(j < SB_QS * i, top > SB_DEAD)

    def body(c):
        j, _ = c
        kb = SB_QS * i - 1 - j
        tile(kb, (j + SB_QS) % 2, ['full'] * SB_QS)
        logits_into(jnp.maximum(kb - 1, 0), (j + SB_QS + 1) % 2, all_subs)
        return j + 1, top_carry()

    lax.while_loop(cond, body, (jnp.int32(0), top_carry()))
    for n, (s, g) in enumerate(units):
        o_ref[q_rows(s), cols(g)] = acc_ref[n].astype(BF16)


def _sb_attention(p, u_tri):
    t = SB_T
    tq = SB_QS * t
    w = SB_G * HEAD_DIM
    ng = SB_HEADS // SB_G
    nu = SB_QS * SB_G
    return pl.pallas_call(
        _sb_kernel,
        grid=(ng, SEQ // tq),
        in_specs=[
            pl.BlockSpec((tq, w), lambda h, i: (i, h)),
            pl.BlockSpec((SEQ, w), lambda h, i: (0, ng + h)),
            pl.BlockSpec((SEQ, w), lambda h, i: (0, 2 * ng + h)),
            pl.BlockSpec((t, t), lambda h, i: (0, 0)),
        ],
        out_specs=pl.BlockSpec((tq, w), lambda h, i: (i, h)),
        out_shape=jax.ShapeDtypeStruct((SEQ, SB_W), BF16),
        scratch_shapes=[pltpu.VMEM((2, nu, t, t), F32), pltpu.VMEM((nu, t, HEAD_DIM), F32),
                        pltpu.VMEM((nu, t, LANES), F32)],
        compiler_params=_params("arbitrary", "arbitrary"),
        name="sb_attention",
    )(p, p, p, u_tri)


DIL_ROWS = 2048


def _stream_rows(start, dilation):
    if dilation == 1:
        return pl.ds(start, DIL_BLK)
    return pl.ds(start, DIL_BLK, stride=dilation)


def _dil_kernel(*refs, dilation, slopes):
    nh = DIL_HEADS_PER_GROUP
    ins = [refs[5 * hh:5 * hh + 5] for hh in range(nh)]
    o_refs = refs[5 * nh:6 * nh]
    l_refs = refs[6 * nh:7 * nh]
    n = pl.program_id(0)
    blk, r = DIL_BLK, dilation
    i_idx = lax.broadcasted_iota(jnp.int32, (blk, 2 * blk), 0)
    j_idx = lax.broadcasted_iota(jnp.int32, (blk, 2 * blk), 1)
    delta = blk + i_idx - j_idx
    in_window = (delta >= 0) & (delta <= blk)
    first_valid = in_window & (j_idx >= jnp.where(n > 0, 0, blk))
    delta_f = delta.astype(F32)
    biases = [(-slopes[hh] * r) * delta_f for hh in range(DIL_HEADS_PER_GROUP)]
    for rho in range(r):
        for b in range(DIL_ROWS // (r * blk)):
            cur = _stream_rows(rho + r * b * blk, r)
            prev = _stream_rows(rho + r * max(b - 1, 0) * blk, r)
            valid = first_valid if b == 0 else in_window
            for hh in range(nh):
                q_ref, kc_ref, kp_ref, vc_ref, vp_ref = ins[hh]
                k_prev = kp_ref[prev, :] if b == 0 else kc_ref[prev, :]
                v_prev = vp_ref[prev, :] if b == 0 else vc_ref[prev, :]
                kk = jnp.concatenate([k_prev, kc_ref[cur, :]], axis=0).astype(BF16)
                vv = jnp.concatenate([v_prev, vc_ref[cur, :]], axis=0).astype(BF16)
                s = _dot_nt(q_ref[cur, :].astype(BF16), kk)
                logits = jnp.where(valid, s + biases[hh], -jnp.inf)
                m = jnp.max(logits, axis=-1, keepdims=True)
                pexp = jnp.exp(logits - m)
                den = jnp.sum(pexp, axis=-1, keepdims=True)
                o_refs[hh][cur, :] = _dot(pexp.astype(BF16), vv) / den
                l_refs[hh][cur, :] = jnp.broadcast_to(m + jnp.log(den), (blk, HEAD_DIM))


def _dil_group(p_dil, g):
    _, r = DIL_GROUPS[g]
    prev_rows = r * DIL_BLK
    ratio = DIL_ROWS // prev_rows
    nh = DIL_HEADS_PER_GROUP
    slopes = tuple(2.0 ** (-8.0 * (g * nh + hh + 1) / DIL_HEADS) for hh in range(nh))
    cur = lambda part, hh: pl.BlockSpec((DIL_ROWS, HEAD_DIM), lambda n: (n, part * DIL_HEADS + g * nh + hh))
    prev = lambda part, hh: pl.BlockSpec((prev_rows, HEAD_DIM),
                                         lambda n: (jnp.maximum(n * ratio - 1, 0), part * DIL_HEADS + g * nh + hh))
    in_specs = []
    for hh in range(nh):
        in_specs += [cur(0, hh), cur(1, hh), prev(1, hh), cur(2, hh), prev(2, hh)]
    out_spec = pl.BlockSpec((DIL_ROWS, HEAD_DIM), lambda n: (n, 0))
    res = pl.pallas_call(
        functools.partial(_dil_kernel, dilation=r, slopes=slopes),
        grid=(SEQ // DIL_ROWS,),
        in_specs=in_specs,
        out_specs=[out_spec] * (2 * nh),
        out_shape=[jax.ShapeDtypeStruct((SEQ, HEAD_DIM), F32)] * (2 * nh),
        compiler_params=_params("arbitrary"),
        name=f"dil_group{g}",
    )(*([p_dil] * (5 * nh)))
    return res[:nh], res[nh:]


def _dil_mix_kernel(*refs):
    ng, nh = len(DIL_GROUPS), DIL_HEADS_PER_GROUP
    out_ref = refs[-1]
    for hh in range(nh):
        o = [refs[g * nh + hh][...] for g in range(ng)]
        lse = [refs[ng * nh + g * nh + hh][...] for g in range(ng)]
        m = functools.reduce(jnp.maximum, lse)
        e = [jnp.exp(x - m) for x in lse]
        num = functools.reduce(lambda a, b: a + b, [ei * oi for ei, oi in zip(e, o)])
        den = functools.reduce(lambda a, b: a + b, e)
        out_ref[:, hh * HEAD_DIM:(hh + 1) * HEAD_DIM] = (num / den).astype(BF16)


def _dil_mix(outs, lses):
    tm = 1024
    flat = [a for grp in outs for a in grp] + [a for grp in lses for a in grp]
    return pl.pallas_call(
        _dil_mix_kernel,
        grid=(SEQ // tm,),
        in_specs=[pl.BlockSpec((tm, HEAD_DIM), lambda i: (i, 0))] * len(flat),
        out_specs=pl.BlockSpec((tm, DIL_OUT_W), lambda i: (i, 0)),
        out_shape=jax.ShapeDtypeStruct((SEQ, DIL_OUT_W), BF16),
        compiler_params=_params("arbitrary"),
        name="dil_mix",
    )(*flat)


def _gla_sum_matrix():
    L = GLA_L
    t = np.arange(L)[:, None]
    s = np.arange(L)[None, :]
    mats = []
    for lev in range(GLA_LEVELS):
        size = 2 << lev
        half = size // 2
        mid = (t // size) * size + half
        later = (t % size) >= half
        mats.append(np.where(later, (s >= mid) & (s <= t), (s > t) & (s < mid)))
    mats.append(s <= t)
    mats.append(s > t)
    return np.concatenate(mats, axis=0).astype(np.float32)


def _gla_kernel(q_ref, k_ref, v0_ref, v1_ref, r0_ref, r1_ref, a_ref, wa_ref, ba_ref, gain_ref, mst_ref,
                o_ref, st_ref):
    L = GLA_L

    @pl.when(pl.program_id(0) == 0)
    def _():
        st_ref[...] = jnp.zeros_like(st_ref)

    row = lax.broadcasted_iota(jnp.int32, (L, L), 0)
    col = lax.broadcasted_iota(jnp.int32, (L, L), 1)
    mst = mst_ref[...]
    gain = gain_ref[...]
    for blk, h in [(b, hh) for b in range(GLA_NB) for hh in range(GLA_HEADS)]:
        rs = slice(blk * L, (blk + 1) * L)
        ks = slice(h * GLA_DK, (h + 1) * GLA_DK)
        a_hi, a_lo = _split2(a_ref[rs, 0:GLA_GATE_RANK])
        wa_hi, wa_lo = _split2(wa_ref[:, ks])
        a = _dot(a_hi, wa_hi) + _dot(a_lo, wa_hi) + _dot(a_hi, wa_lo) + ba_ref[:, ks]
        log_a = _log_sigmoid(a) * (1.0 / GLA_GATE_TAU)
        e_all = _dot(mst, log_a.astype(BF16))
        qb = q_ref[rs, ks]
        kb = k_ref[rs, ks]
        q = qb.astype(F32)
        k = kb.astype(F32)
        scores = jnp.where(row == col, _dot_nt(qb, kb), 0.0)
        for lev in range(GLA_LEVELS):
            size = 2 << lev
            e = jnp.exp(e_all[lev * L:(lev + 1) * L])
            later = (row & (size // 2)) != 0
            qt = jnp.where(later, q * e, 0.0).astype(BF16)
            kt = jnp.where(later, 0.0, k * e).astype(BF16)
            part = _dot_nt(qt, kt)
            if size < L:
                part = jnp.where((row >> (lev + 1)) == (col >> (lev + 1)), part, 0.0)
            scores = scores + part
        e_cum = e_all[GLA_LEVELS * L:(GLA_LEVELS + 1) * L]
        e_rev = e_all[(GLA_LEVELS + 1) * L:(GLA_LEVELS + 2) * L]
        total = e_cum[L - 1:L]
        q_in = (q * jnp.exp(e_cum)).astype(BF16)
        k_out = (k * jnp.exp(e_rev)).astype(BF16)
        v_src = v0_ref if h < 2 else v1_ref
        r_src = r0_ref if h < 2 else r1_ref
        vs = slice((h % 2) * GLA_DV, (h % 2 + 1) * GLA_DV)
        v = v_src[rs, vs]
        st = st_ref[h]
        o = _dot_nt(q_in, st.astype(BF16)) + _dot(scores.astype(BF16), v)
        o = o * (GLA_DK ** -0.5)
        v_t = v.astype(F32).T.astype(BF16)
        st_ref[h] = st * jnp.exp(total) + _dot(v_t, k_out)
        ms = jnp.mean(o * o, axis=-1, keepdims=True)
        y = o * lax.rsqrt(ms + EPS) * gain
        r = r_src[rs, vs].astype(F32)
        o_ref[rs, h * GLA_DV:(h + 1) * GLA_DV] = (y * (r * jax.nn.sigmoid(r))).astype(BF16)


def _gla(p, a_cols, w_gla_a, b_gla_a, gla_out_gain, mst, l):
    L = GLA_L
    rows = GLA_NB * L
    w = 512
    col = lambda c: pl.BlockSpec((rows, w), lambda i: (i, c))
    return pl.pallas_call(
        _gla_kernel,
        grid=(SEQ // rows,),
        in_specs=[col(0), col(1), col(2), col(3), col(4), col(5),
                  pl.BlockSpec((rows, LANES), lambda i: (i, 0)),
                  pl.BlockSpec((None, GLA_GATE_RANK, GLA_QK_W), lambda i: (l, 0, 0)),
                  pl.BlockSpec((None, 1, GLA_QK_W), lambda i: (l, 0, 0)),
                  pl.BlockSpec((None, 1, GLA_DV), lambda i: (l, 0, 0)),
                  pl.BlockSpec(((GLA_LEVELS + 2) * L, L), lambda i: (0, 0))],
        out_specs=pl.BlockSpec((rows, GLA_V_W), lambda i: (i, 0)),
        out_shape=jax.ShapeDtypeStruct((SEQ, GLA_V_W), BF16),
        scratch_shapes=[pltpu.VMEM((GLA_HEADS, GLA_DV, GLA_DK), F32)],
        compiler_params=_params("arbitrary"),
        name="gla",
    )(p, p, p, p, p, p, a_cols, w_gla_a, b_gla_a.reshape(DEPTH, 1, GLA_QK_W),
      gla_out_gain.reshape(DEPTH, 1, GLA_DV), mst)


def _qk_gains(q_gain, k_gain, heads, q_scale):
    ones = jnp.ones((heads * HEAD_DIM,), F32)
    return jnp.concatenate([jnp.tile(q_gain * q_scale, heads), jnp.tile(k_gain, heads), ones]).reshape(1, -1)


def kernel(x, c, w_ada, b_ada, norm1_gain, norm2_gain, w_in, sb_q_gain, sb_k_gain, dil_q_gain, dil_k_gain,
           w_gla_a, b_gla_a, gla_out_gain, w_br_sb, w_br_dil, w_br_gla, w_out, w_ffn_in, w_ffn_out):
    assert x.shape == (1, SEQ, D_MODEL) and c.shape == (1, D_MODEL)
    xs = x.reshape(SEQ, D_MODEL)
    mod = _ada(c, w_ada, b_ada)
    u_tri = jnp.asarray(np.tril(np.ones((SB_T, SB_T), np.float32), -1), BF16)
    mst = jnp.asarray(_gla_sum_matrix(), BF16)
    scale = HEAD_DIM ** -0.5
    w_in_t = jnp.swapaxes(w_in, 1, 2)
    for l in range(DEPTH):
        h = _norm_mod(xs, norm1_gain, mod, l, 0, 1)
        p_sb = _inproj_qkv(h, w_in_t, l, OFF_SB, _qk_gains(sb_q_gain[l], sb_k_gain[l], SB_HEADS, scale * LOG2E),
                           BF16, "inproj_sb")
        p_dil = _inproj_qkv(h, w_in_t, l, OFF_DIL, _qk_gains(dil_q_gain[l], dil_k_gain[l], DIL_HEADS, scale),
                            F32, "inproj_dil")
        p = _inproj_plain(h, w_in_t, l, OFF_GLA, SEG_GLA_W, IN_TN, BF16, "inproj_gla")
        a_cols = _inproj_plain(h, w_in_t, l, OFF_A, LANES, LANES, F32, "inproj_a")
        gates = _inproj_gates(h, w_in_t, l)
        o_sb = _sb_attention(p_sb, u_tri)
        dil = [_dil_group(p_dil, g) for g in range(len(DIL_GROUPS))]
        o_dil = _dil_mix([d[0] for d in dil], [d[1] for d in dil])
        o_gla = _gla(p, a_cols, w_gla_a, b_gla_a, gla_out_gain, mst, l)
        y = _branch(o_sb, o_dil, o_gla, w_br_sb, w_br_dil, w_br_gla, gates, l)
        xs, h2 = _resid_proj_norm(y, w_out, xs, mod, l, 2, norm2_gain, 3, 4)
        u = _ffn_in(h2, w_ffn_in, l)
        xs = _resid_proj(u, w_ffn_out, xs, mod, l, 5, 512, 512)
    return xs.reshape(1, SEQ, D_MODEL)
```

```python
import functools
import math

import numpy as np
import jax
import jax.numpy as jnp
from jax import lax
from jax.experimental import pallas as pl
from jax.experimental.pallas import tpu as pltpu

F32 = jnp.float32
BF16 = jnp.bfloat16

D_MODEL = 2048
SEQ = 8192
DEPTH = 2
HEAD_DIM = 128
SB_HEADS = 6
DIL_GROUPS = ((128, 1), (512, 4), (2048, 16))
DIL_HEADS_PER_GROUP = 2
DIL_HEADS = DIL_HEADS_PER_GROUP * len(DIL_GROUPS)
GLA_HEADS = 4
GLA_DK = 128
GLA_DV = 256
GLA_GATE_RANK = 16
GLA_GATE_TAU = 16.0
N_BRANCH = 3
D_FF = 5632
N_MOD = 6
EPS = 1e-6

SB_W = SB_HEADS * HEAD_DIM
DIL_W = DIL_HEADS * HEAD_DIM
DIL_OUT_W = DIL_HEADS_PER_GROUP * HEAD_DIM
GLA_QK_W = GLA_HEADS * GLA_DK
GLA_V_W = GLA_HEADS * GLA_DV
OFF_SB = 0
OFF_DIL = 3 * SB_W
OFF_GLA = OFF_DIL + 3 * DIL_W
OFF_A = OFF_GLA + 2 * GLA_QK_W + 2 * GLA_V_W
OFF_GATE = OFF_A + GLA_GATE_RANK
SEG_SB_W = 3 * SB_W
SEG_DIL_W = 3 * DIL_W
SEG_GLA_W = 2 * GLA_QK_W + 2 * GLA_V_W

LANES = 128
VMEM_LIMIT = 56 * 1024 * 1024

MM_TM = 1024
IN_TN = 768
SB_T = 256
SB_G = 3
SB_QS = 2
LOG2E = 1.4426950408889634
SB_DEAD = -160.0
DIL_BLK = 128
GLA_L = 128
GLA_NB = 2
GLA_LEVELS = 7


def _params(*sem):
    return pltpu.CompilerParams(dimension_semantics=sem, vmem_limit_bytes=VMEM_LIMIT)


def _dot(a, b):
    return jnp.dot(a, b, preferred_element_type=F32)


def _dot_nt(a, b):
    return lax.dot_general(a, b, (((1,), (1,)), ((), ())), preferred_element_type=F32)


def _split2(x):
    hi = x.astype(BF16)
    lo = (x - hi.astype(F32)).astype(BF16)
    return hi, lo


def _split3(x):
    p1 = x.astype(BF16)
    r1 = x - p1.astype(F32)
    p2 = r1.astype(BF16)
    p3 = (r1 - p2.astype(F32)).astype(BF16)
    return p1, p2, p3


def _sigmoid(x):
    return 0.5 * jnp.tanh(0.5 * x) + 0.5


def _log_sigmoid(z):
    return jnp.minimum(z, 0.0) - jnp.log(1.0 + jnp.exp(-jnp.abs(z)))


def _ada_kernel(c_ref, w_ref, b_ref, o_ref):
    c = c_ref[...]
    s = c * jax.nn.sigmoid(c)
    o_ref[...] = jnp.sum(w_ref[...] * s, axis=0, keepdims=True) + b_ref[...]


def _ada(c, w_ada, b_ada):
    tn = 1024
    n = N_MOD * D_MODEL
    return pl.pallas_call(
        _ada_kernel,
        grid=(DEPTH, n // tn),
        in_specs=[
            pl.BlockSpec((D_MODEL, 1), lambda l, j: (0, 0)),
            pl.BlockSpec((None, D_MODEL, tn), lambda l, j: (l, 0, j)),
            pl.BlockSpec((None, 1, tn), lambda l, j: (l, 0, j)),
        ],
        out_specs=pl.BlockSpec((None, 1, tn), lambda l, j: (l, 0, j)),
        out_shape=jax.ShapeDtypeStruct((DEPTH, 1, n), F32),
        compiler_params=_params("arbitrary", "arbitrary"),
        name="ada_mod",
    )(c.reshape(D_MODEL, 1), w_ada, b_ada.reshape(DEPTH, 1, n))


def _norm_kernel(x_ref, g_ref, sc_ref, sh_ref, o_ref):
    x = x_ref[...]
    ms = jnp.mean(x * x, axis=-1, keepdims=True)
    y = x * lax.rsqrt(ms + EPS) * g_ref[...]
    o_ref[...] = (y * (1.0 + sc_ref[...]) + sh_ref[...]).astype(BF16)


def _norm_mod(x, gain, mod, l, shift_idx, scale_idx):
    tm = 512
    return pl.pallas_call(
        _norm_kernel,
        grid=(SEQ // tm,),
        in_specs=[
            pl.BlockSpec((tm, D_MODEL), lambda i: (i, 0)),
            pl.BlockSpec((None, 1, D_MODEL), lambda i: (l, 0, 0)),
            pl.BlockSpec((None, 1, D_MODEL), lambda i: (l, 0, scale_idx)),
            pl.BlockSpec((None, 1, D_MODEL), lambda i: (l, 0, shift_idx)),
        ],
        out_specs=pl.BlockSpec((tm, D_MODEL), lambda i: (i, 0)),
        out_shape=jax.ShapeDtypeStruct((SEQ, D_MODEL), BF16),
        compiler_params=_params("arbitrary"),
        name="norm_mod",
    )(x, gain.reshape(DEPTH, 1, D_MODEL), mod, mod)


def _cast_weights(pairs):
    @pl.when(pl.program_id(1) == 0)
    def _():
        for w_ref, wb_ref in pairs:
            wb_ref[...] = w_ref[...].astype(BF16)


MXU_N = 256


def _col_chunks(tn):
    return [slice(c, min(c + MXU_N, tn)) for c in range(0, tn, MXU_N)]


XPOSE_CHUNK = 256


def _cast_weights_t(wt_ref, wb_ref, skip=0, tail_ref=None):
    @pl.when(pl.program_id(1) == 0)
    def _():
        for c in range(D_MODEL // XPOSE_CHUNK):
            cs = slice(c * XPOSE_CHUNK, (c + 1) * XPOSE_CHUNK)
            blk = wt_ref[skip:, cs]
            if tail_ref is not None:
                blk = jnp.concatenate([blk, tail_ref[:, cs]], axis=0)
            wb_ref[cs, :] = blk.T.astype(BF16)


def _inproj_qkv_kernel(a_ref, w_ref, g_ref, o_ref, wb_ref):
    _cast_weights_t(w_ref, wb_ref)
    is_norm = pl.program_id(0) < 2
    a = a_ref[...]

    @pl.when(is_norm)
    def _():
        g = g_ref[...]
        for ns in _col_chunks(IN_TN):
            acc = _dot(a, wb_ref[:, ns])
            for c in range(MXU_N // HEAD_DIM):
                cs = slice(ns.start + c * HEAD_DIM, ns.start + (c + 1) * HEAD_DIM)
                blk = acc[:, c * HEAD_DIM:(c + 1) * HEAD_DIM]
                ms = jnp.mean(blk * blk, axis=-1, keepdims=True)
                o_ref[:, cs] = (blk * lax.rsqrt(ms + EPS) * g[:, cs]).astype(o_ref.dtype)

    @pl.when(jnp.logical_not(is_norm))
    def _():
        for ns in _col_chunks(IN_TN):
            o_ref[:, ns] = _dot(a, wb_ref[:, ns]).astype(o_ref.dtype)


def _inproj_qkv(h, w_in_t, l, col0, gains, out_dtype, name):
    tm, tn = MM_TM, IN_TN
    width = 3 * tn
    return pl.pallas_call(
        _inproj_qkv_kernel,
        grid=(width // tn, SEQ // tm),
        in_specs=[
            pl.BlockSpec((tm, D_MODEL), lambda j, m: (m, 0)),
            pl.BlockSpec((None, tn, D_MODEL), lambda j, m: (l, col0 // tn + j, 0)),
            pl.BlockSpec((1, tn), lambda j, m: (0, j)),
        ],
        out_specs=pl.BlockSpec((tm, tn), lambda j, m: (m, j)),
        out_shape=jax.ShapeDtypeStruct((SEQ, width), out_dtype),
        scratch_shapes=[pltpu.VMEM((D_MODEL, tn), BF16)],
        compiler_params=_params("arbitrary", "arbitrary"),
        name=name,
    )(h, w_in_t, gains)


def _plain_kernel(a_ref, w_ref, o_ref, wb_ref):
    _cast_weights_t(w_ref, wb_ref)
    a = a_ref[...]
    for ns in _col_chunks(o_ref.shape[1]):
        o_ref[:, ns] = _dot(a, wb_ref[:, ns]).astype(o_ref.dtype)


def _inproj_plain(h, w_in_t, l, col0, width, tn, out_dtype, name):
    tm = MM_TM
    return pl.pallas_call(
        _plain_kernel,
        grid=(width // tn, SEQ // tm),
        in_specs=[
            pl.BlockSpec((tm, D_MODEL), lambda j, m: (m, 0)),
            pl.BlockSpec((None, tn, D_MODEL), lambda j, m: (l, col0 // tn + j, 0)),
        ],
        out_specs=pl.BlockSpec((tm, tn), lambda j, m: (m, j)),
        out_shape=jax.ShapeDtypeStruct((SEQ, width), out_dtype),
        scratch_shapes=[pltpu.VMEM((D_MODEL, tn), BF16)],
        compiler_params=_params("arbitrary", "arbitrary"),
        name=name,
    )(h, w_in_t)


GATE_SHIFT = OFF_GATE - OFF_A


def _gates_kernel(a_ref, wm_ref, wx_ref, o_ref, wb_ref):
    _cast_weights_t(wm_ref, wb_ref, skip=GATE_SHIFT, tail_ref=wx_ref)
    a = a_ref[...]
    for ns in _col_chunks(IN_TN):
        o_ref[:, ns] = _sigmoid(_dot(a, wb_ref[:, ns])).astype(BF16)


def _inproj_gates(h, w_in_t, l):
    tm, tn = MM_TM, IN_TN
    n = N_BRANCH * D_MODEL
    return pl.pallas_call(
        _gates_kernel,
        grid=(n // tn, SEQ // tm),
        in_specs=[
            pl.BlockSpec((tm, D_MODEL), lambda j, m: (m, 0)),
            pl.BlockSpec((None, tn, D_MODEL), lambda j, m: (l, OFF_A // tn + j, 0)),
            pl.BlockSpec((None, GATE_SHIFT, D_MODEL), lambda j, m: (l, (OFF_A + (j + 1) * tn) // GATE_SHIFT, 0)),
        ],
        out_specs=pl.BlockSpec((tm, tn), lambda j, m: (m, j)),
        out_shape=jax.ShapeDtypeStruct((SEQ, n), BF16),
        scratch_shapes=[pltpu.VMEM((D_MODEL, tn), BF16)],
        compiler_params=_params("arbitrary", "arbitrary"),
        name="inproj_gates",
    )(h, w_in_t, w_in_t)


def _branch_kernel(a0, a1, a2, w0, w1, w2, g0, g1, g2, o_ref, wb0, wb1, wb2):
    _cast_weights([(w0, wb0), (w1, wb1), (w2, wb2)])
    x0, x1, x2 = a0[...], a1[...], a2[...]
    for ns in _col_chunks(o_ref.shape[1]):
        y = g0[:, ns].astype(F32) * _dot(x0, wb0[:, ns])
        y = y + g1[:, ns].astype(F32) * _dot(x1, wb1[:, ns])
        y = y + g2[:, ns].astype(F32) * _dot(x2, wb2[:, ns])
        o_ref[:, ns] = y.astype(BF16)


def _branch(o_sb, o_dil, o_gla, w_br_sb, w_br_dil, w_br_gla, gates, l):
    tm, tn = MM_TM, 1024
    nb = D_MODEL // tn
    a_spec = lambda w: pl.BlockSpec((tm, w), lambda j, m: (m, 0))
    w_spec = lambda w: pl.BlockSpec((None, w, tn), lambda j, m: (l, 0, j))
    g_spec = lambda b: pl.BlockSpec((tm, tn), lambda j, m: (m, b * nb + j))
    return pl.pallas_call(
        _branch_kernel,
        grid=(nb, SEQ // tm),
        in_specs=[a_spec(SB_W), a_spec(DIL_OUT_W), a_spec(GLA_V_W),
                  w_spec(SB_W), w_spec(DIL_OUT_W), w_spec(GLA_V_W),
                  g_spec(0), g_spec(1), g_spec(2)],
        out_specs=pl.BlockSpec((tm, tn), lambda j, m: (m, j)),
        out_shape=jax.ShapeDtypeStruct((SEQ, D_MODEL), BF16),
        scratch_shapes=[pltpu.VMEM((SB_W, tn), BF16), pltpu.VMEM((DIL_OUT_W, tn), BF16),
                        pltpu.VMEM((GLA_V_W, tn), BF16)],
        compiler_params=_params("arbitrary", "arbitrary"),
        name="branch_proj",
    )(o_sb, o_dil, o_gla, w_br_sb, w_br_dil, w_br_gla, gates, gates, gates)


def _resid_kernel(a_ref, w_ref, x_ref, gate_ref, o_ref, wb_ref):
    _cast_weights([(w_ref, wb_ref)])
    a = a_ref[...]
    for ns in _col_chunks(o_ref.shape[1]):
        o_ref[:, ns] = x_ref[:, ns] + gate_ref[:, ns] * _dot(a, wb_ref[:, ns])


def _resid_proj(a, w, x, mod, l, gate_idx, tm, tn):
    k = a.shape[1]
    nb = D_MODEL // tn
    return pl.pallas_call(
        _resid_kernel,
        grid=(nb, SEQ // tm),
        in_specs=[
            pl.BlockSpec((tm, k), lambda j, m: (m, 0)),
            pl.BlockSpec((None, k, tn), lambda j, m: (l, 0, j)),
            pl.BlockSpec((tm, tn), lambda j, m: (m, j)),
            pl.BlockSpec((None, 1, tn), lambda j, m: (l, 0, gate_idx * nb + j)),
        ],
        out_specs=pl.BlockSpec((tm, tn), lambda j, m: (m, j)),
        out_shape=jax.ShapeDtypeStruct((SEQ, D_MODEL), F32),
        scratch_shapes=[pltpu.VMEM((k, tn), BF16)],
        compiler_params=_params("arbitrary", "arbitrary"),
        name="resid_proj",
    )(a, w, x, mod)


def _resid_norm_kernel(a_ref, w_ref, x_ref, gate_ref, g_ref, sc_ref, sh_ref, o_ref, h_ref, wb_ref):
    @pl.when(pl.program_id(0) == 0)
    def _():
        wb_ref[...] = w_ref[...].astype(BF16)

    a = a_ref[...]
    for ns in _col_chunks(D_MODEL):
        o_ref[:, ns] = x_ref[:, ns] + gate_ref[:, ns] * _dot(a, wb_ref[:, ns])
    x = o_ref[...]
    ms = jnp.mean(x * x, axis=-1, keepdims=True)
    y = x * lax.rsqrt(ms + EPS) * g_ref[...]
    h_ref[...] = (y * (1.0 + sc_ref[...]) + sh_ref[...]).astype(BF16)


def _resid_proj_norm(a, w, x, mod, l, gate_idx, gain, shift_idx, scale_idx):
    tm = 512
    k = a.shape[1]
    row = lambda idx: pl.BlockSpec((None, 1, D_MODEL), lambda m: (l, 0, idx))
    return pl.pallas_call(
        _resid_norm_kernel,
        grid=(SEQ // tm,),
        in_specs=[
            pl.BlockSpec((tm, k), lambda m: (m, 0)),
            pl.BlockSpec((None, k, D_MODEL), lambda m: (l, 0, 0), pipeline_mode=pl.Buffered(1)),
            pl.BlockSpec((tm, D_MODEL), lambda m: (m, 0)),
            row(gate_idx), row(0), row(scale_idx), row(shift_idx),
        ],
        out_specs=[pl.BlockSpec((tm, D_MODEL), lambda m: (m, 0)), pl.BlockSpec((tm, D_MODEL), lambda m: (m, 0))],
        out_shape=[jax.ShapeDtypeStruct((SEQ, D_MODEL), F32), jax.ShapeDtypeStruct((SEQ, D_MODEL), BF16)],
        scratch_shapes=[pltpu.VMEM((k, D_MODEL), BF16)],
        compiler_params=_params("arbitrary"),
        name="resid_proj_norm",
    )(a, w, x, mod, gain.reshape(DEPTH, 1, D_MODEL), mod, mod)


def _ffn_in_kernel(a_ref, wg_ref, wu_ref, o_ref, wgb_ref, wub_ref):
    _cast_weights([(wg_ref, wgb_ref), (wu_ref, wub_ref)])
    a = a_ref[...]
    for ns in _col_chunks(o_ref.shape[1]):
        g = _dot(a, wgb_ref[:, ns])
        u = _dot(a, wub_ref[:, ns])
        o_ref[:, ns] = (g * _sigmoid(g) * u).astype(BF16)


def _ffn_in(h, w_ffn_in, l):
    tm, tn = MM_TM, 512
    nb = D_FF // tn
    return pl.pallas_call(
        _ffn_in_kernel,
        grid=(nb, SEQ // tm),
        in_specs=[
            pl.BlockSpec((tm, D_MODEL), lambda j, m: (m, 0)),
            pl.BlockSpec((None, D_MODEL, tn), lambda j, m: (l, 0, j)),
            pl.BlockSpec((None, D_MODEL, tn), lambda j, m: (l, 0, nb + j)),
        ],
        out_specs=pl.BlockSpec((tm, tn), lambda j, m: (m, j)),
        out_shape=jax.ShapeDtypeStruct((SEQ, D_FF), BF16),
        scratch_shapes=[pltpu.VMEM((D_MODEL, tn), BF16), pltpu.VMEM((D_MODEL, tn), BF16)],
        compiler_params=_params("arbitrary", "arbitrary"),
        name="ffn_in",
    )(h, w_ffn_in, w_ffn_in)


def _sb_kernel(q_ref, k_ref, v_ref, u_ref, o_ref, z_ref, acc_ref, carry_ref):
    t = SB_T
    i = pl.program_id(1)
    u = u_ref[...]
    units = [(s, g) for s in range(SB_QS) for g in range(SB_G)]

    def q_rows(s):
        return slice(s * t, (s + 1) * t)

    def cols(g):
        return slice(g * HEAD_DIM, (g + 1) * HEAD_DIM)

    def logits_into(kb, slot, subs):
        ks = pl.multiple_of(kb * t, t)
        for n, (s, g) in enumerate(units):
            if s in subs:
                z_ref[slot, n] = _dot_nt(q_ref[q_rows(s), cols(g)], k_ref[pl.ds(ks, t), cols(g)])

    def tile(kb, slot, kinds):
        ks = pl.multiple_of(kb * t, t)
        if 'diag' in kinds:
            row = lax.broadcasted_iota(jnp.int32, (t, t), 0)
            col = lax.broadcasted_iota(jnp.int32, (t, t), 1)
            causal = col < row
        for n, (s, g) in enumerate(units):
            if kinds[s] is None:
                continue
            diag = kinds[s] == 'diag'
            z = z_ref[slot, n]
            ls = jnp.minimum(z, 0.0) - jnp.log(1.0 + jnp.exp2(-jnp.abs(z))) * LOG2E
            lr = ls - z
            if diag:
                lr = jnp.where(causal, lr, 0.0)
            carry = carry_ref[n]
            after = _dot(lr.astype(BF16), u) + jnp.concatenate([carry] * (t // LANES), axis=1)
            w = jnp.exp2(ls + after)
            if diag:
                w = jnp.where(causal, w, 0.0)
            acc_ref[n] += _dot(w.astype(BF16), v_ref[pl.ds(ks, t), cols(g)])
            carry_ref[n] = carry + jnp.sum(lr, axis=-1, keepdims=True)

    acc_ref[...] = jnp.zeros_like(acc_ref)
    carry_ref[...] = jnp.zeros_like(carry_ref)
    all_subs = tuple(range(SB_QS))
    top = SB_QS * i + SB_QS - 1
    logits_into(top, 0, (SB_QS - 1,))
    for d in range(SB_QS):
        kinds = [None if s < SB_QS - 1 - d else ('diag' if s == SB_QS - 1 - d else 'full') for s in all_subs]
        tile(top - d, d % 2, kinds)
        nxt = tuple(s for s in all_subs if s >= SB_QS - 2 - d) if d < SB_QS - 1 else all_subs
        logits_into(jnp.maximum(top - d - 1, 0), (d + 1) % 2, nxt)

    def top_carry():
        return jnp.max(carry_ref[...])

    def cond(c):
        j, top = c
        return jnp.logical_and(j < SB_QS * i, top > SB_DEAD)

    def body(c):
        j, _ = c
        kb = SB_QS * i - 1 - j
        tile(kb, (j + SB_QS) % 2, ['full'] * SB_QS)
        logits_into(jnp.maximum(kb - 1, 0), (j + SB_QS + 1) % 2, all_subs)
        return j + 1, top_carry()

    lax.while_loop(cond, body, (jnp.int32(0), top_carry()))
    for n, (s, g) in enumerate(units):
        o_ref[q_rows(s), cols(g)] = acc_ref[n].astype(BF16)


def _sb_attention(p, u_tri):
    t = SB_T
    tq = SB_QS * t
    w = SB_G * HEAD_DIM
    ng = SB_HEADS // SB_G
    nu = SB_QS * SB_G
    return pl.pallas_call(
        _sb_kernel,
        grid=(ng, SEQ // tq),
        in_specs=[
            pl.BlockSpec((tq, w), lambda h, i: (i, h)),
            pl.BlockSpec((SEQ, w), lambda h, i: (0, ng + h)),
            pl.BlockSpec((SEQ, w), lambda h, i: (0, 2 * ng + h)),
            pl.BlockSpec((t, t), lambda h, i: (0, 0)),
        ],
        out_specs=pl.BlockSpec((tq, w), lambda h, i: (i, h)),
        out_shape=jax.ShapeDtypeStruct((SEQ, SB_W), BF16),
        scratch_shapes=[pltpu.VMEM((2, nu, t, t), F32), pltpu.VMEM((nu, t, HEAD_DIM), F32),
                        pltpu.VMEM((nu, t, LANES), F32)],
        compiler_params=_params("arbitrary", "arbitrary"),
        name="sb_attention",
    )(p, p, p, u_tri)


DIL_ROWS = 2048


def _stream_rows(start, dilation):
    if dilation == 1:
        return pl.ds(start, DIL_BLK)
    return pl.ds(start, DIL_BLK, stride=dilation)


def _dil_kernel(*refs, dilation, slopes):
    nh = DIL_HEADS_PER_GROUP
    ins = [refs[5 * hh:5 * hh + 5] for hh in range(nh)]
    o_refs = refs[5 * nh:6 * nh]
    l_refs = refs[6 * nh:7 * nh]
    n = pl.program_id(0)
    blk, r = DIL_BLK, dilation
    i_idx = lax.broadcasted_iota(jnp.int32, (blk, 2 * blk), 0)
    j_idx = lax.broadcasted_iota(jnp.int32, (blk, 2 * blk), 1)
    delta = blk + i_idx - j_idx
    in_window = (delta >= 0) & (delta <= blk)
    first_valid = in_window & (j_idx >= jnp.where(n > 0, 0, blk))
    delta_f = delta.astype(F32)
    biases = [(-slopes[hh] * r) * delta_f for hh in range(DIL_HEADS_PER_GROUP)]
    for rho in range(r):
        for b in range(DIL_ROWS // (r * blk)):
            cur = _stream_rows(rho + r * b * blk, r)
            prev = _stream_rows(rho + r * max(b - 1, 0) * blk, r)
            valid = first_valid if b == 0 else in_window
            for hh in range(nh):
                q_ref, kc_ref, kp_ref, vc_ref, vp_ref = ins[hh]
                k_prev = kp_ref[prev, :] if b == 0 else kc_ref[prev, :]
                v_prev = vp_ref[prev, :] if b == 0 else vc_ref[prev, :]
                kk = jnp.concatenate([k_prev, kc_ref[cur, :]], axis=0).astype(BF16)
                vv = jnp.concatenate([v_prev, vc_ref[cur, :]], axis=0).astype(BF16)
                s = _dot_nt(q_ref[cur, :].astype(BF16), kk)
                logits = jnp.where(valid, s + biases[hh], -jnp.inf)
                m = jnp.max(logits, axis=-1, keepdims=True)
                pexp = jnp.exp(logits - m)
                den = jnp.sum(pexp, axis=-1, keepdims=True)
                o_refs[hh][cur, :] = _dot(pexp.astype(BF16), vv) / den
                l_refs[hh][cur, :] = jnp.broadcast_to(m + jnp.log(den), (blk, HEAD_DIM))


def _dil_group(p_dil, g):
    _, r = DIL_GROUPS[g]
    prev_rows = r * DIL_BLK
    ratio = DIL_ROWS // prev_rows
    nh = DIL_HEADS_PER_GROUP
    slopes = tuple(2.0 ** (-8.0 * (g * nh + hh + 1) / DIL_HEADS) for hh in range(nh))
    cur = lambda part, hh: pl.BlockSpec((DIL_ROWS, HEAD_DIM), lambda n: (n, part * DIL_HEADS + g * nh + hh))
    prev = lambda part, hh: pl.BlockSpec((prev_rows, HEAD_DIM),
                                         lambda n: (jnp.maximum(n * ratio - 1, 0), part * DIL_HEADS + g * nh + hh))
    in_specs = []
    for hh in range(nh):
        in_specs += [cur(0, hh), cur(1, hh), prev(1, hh), cur(2, hh), prev(2, hh)]
    out_spec = pl.BlockSpec((DIL_ROWS, HEAD_DIM), lambda n: (n, 0))
    res = pl.pallas_call(
        functools.partial(_dil_kernel, dilation=r, slopes=slopes),
        grid=(SEQ // DIL_ROWS,),
        in_specs=in_specs,
        out_specs=[out_spec] * (2 * nh),
        out_shape=[jax.ShapeDtypeStruct((SEQ, HEAD_DIM), F32)] * (2 * nh),
        compiler_params=_params("arbitrary"),
        name=f"dil_group{g}",
    )(*([p_dil] * (5 * nh)))
    return res[:nh], res[nh:]


def _dil_mix_kernel(*refs):
    ng, nh = len(DIL_GROUPS), DIL_HEADS_PER_GROUP
    out_ref = refs[-1]
    for hh in range(nh):
        o = [refs[g * nh + hh][...] for g in range(ng)]
        lse = [refs[ng * nh + g * nh + hh][...] for g in range(ng)]
        m = functools.reduce(jnp.maximum, lse)
        e = [jnp.exp(x - m) for x in lse]
        num = functools.reduce(lambda a, b: a + b, [ei * oi for ei, oi in zip(e, o)])
        den = functools.reduce(lambda a, b: a + b, e)
        out_ref[:, hh * HEAD_DIM:(hh + 1) * HEAD_DIM] = (num / den).astype(BF16)


def _dil_mix(outs, lses):
    tm = 1024
    flat = [a for grp in outs for a in grp] + [a for grp in lses for a in grp]
    return pl.pallas_call(
        _dil_mix_kernel,
        grid=(SEQ // tm,),
        in_specs=[pl.BlockSpec((tm, HEAD_DIM), lambda i: (i, 0))] * len(flat),
        out_specs=pl.BlockSpec((tm, DIL_OUT_W), lambda i: (i, 0)),
        out_shape=jax.ShapeDtypeStruct((SEQ, DIL_OUT_W), BF16),
        compiler_params=_params("arbitrary"),
        name="dil_mix",
    )(*flat)


def _gla_sum_matrix():
    L = GLA_L
    t = np.arange(L)[:, None]
    s = np.arange(L)[None, :]
    mats = []
    for lev in range(GLA_LEVELS):
        size = 2 << lev
        half = size // 2
        mid = (t // size) * size + half
        later = (t % size) >= half
        mats.append(np.where(later, (s >= mid) & (s <= t), (s > t) & (s < mid)))
    mats.append(s <= t)
    mats.append(s > t)
    return np.concatenate(mats, axis=0).astype(np.float32)


def _gla_kernel(q_ref, k_ref, v0_ref, v1_ref, r0_ref, r1_ref, a_ref, wa_ref, ba_ref, gain_ref, mst_ref,
                o_ref, st_ref):
    L = GLA_L

    @pl.when(pl.program_id(0) == 0)
    def _():
        st_ref[...] = jnp.zeros_like(st_ref)

    row = lax.broadcasted_iota(jnp.int32, (L, L), 0)
    col = lax.broadcasted_iota(jnp.int32, (L, L), 1)
    mst = mst_ref[...]
    gain = gain_ref[...]
    for blk, h in [(b, hh) for b in range(GLA_NB) for hh in range(GLA_HEADS)]:
        rs = slice(blk * L, (blk + 1) * L)
        ks = slice(h * GLA_DK, (h + 1) * GLA_DK)
        a_hi, a_lo = _split2(a_ref[rs, 0:GLA_GATE_RANK])
        wa_hi, wa_lo = _split2(wa_ref[:, ks])
        a = _dot(a_hi, wa_hi) + _dot(a_lo, wa_hi) + _dot(a_hi, wa_lo) + ba_ref[:, ks]
        log_a = _log_sigmoid(a) * (1.0 / GLA_GATE_TAU)
        e_all = _dot(mst, log_a.astype(BF16))
        qb = q_ref[rs, ks]
        kb = k_ref[rs, ks]
        q = qb.astype(F32)
        k = kb.astype(F32)
        scores = jnp.where(row == col, _dot_nt(qb, kb), 0.0)
        for lev in range(GLA_LEVELS):
            size = 2 << lev
            e = jnp.exp(e_all[lev * L:(lev + 1) * L])
            later = (row & (size // 2)) != 0
            qt = jnp.where(later, q * e, 0.0).astype(BF16)
            kt = jnp.where(later, 0.0, k * e).astype(BF16)
            part = _dot_nt(qt, kt)
            if size < L:
                part = jnp.where((row >> (lev + 1)) == (col >> (lev + 1)), part, 0.0)
            scores = scores + part
        e_cum = e_all[GLA_LEVELS * L:(GLA_LEVELS + 1) * L]
        e_rev = e_all[(GLA_LEVELS + 1) * L:(GLA_LEVELS + 2) * L]
        total = e_cum[L - 1:L]
        q_in = (q * jnp.exp(e_cum)).astype(BF16)
        k_out = (k * jnp.exp(e_rev)).astype(BF16)
        v_src = v0_ref if h < 2 else v1_ref
        r_src = r0_ref if h < 2 else r1_ref
        vs = slice((h % 2) * GLA_DV, (h % 2 + 1) * GLA_DV)
        v = v_src[rs, vs]
        st = st_ref[h]
        o = _dot_nt(q_in, st.astype(BF16)) + _dot(scores.astype(BF16), v)
        o = o * (GLA_DK ** -0.5)
        v_t = v.astype(F32).T.astype(BF16)
        st_ref[h] = st * jnp.exp(total) + _dot(v_t, k_out)
        ms = jnp.mean(o * o, axis=-1, keepdims=True)
        y = o * lax.rsqrt(ms + EPS) * gain
        r = r_src[rs, vs].astype(F32)
        o_ref[rs, h * GLA_DV:(h + 1) * GLA_DV] = (y * (r * _sigmoid(r))).astype(BF16)


def _gla(p, a_cols, w_gla_a, b_gla_a, gla_out_gain, mst, l):
    L = GLA_L
    rows = GLA_NB * L
    w = 512
    col = lambda c: pl.BlockSpec((rows, w), lambda i: (i, c))
    return pl.pallas_call(
        _gla_kernel,
        grid=(SEQ // rows,),
        in_specs=[col(0), col(1), col(2), col(3), col(4), col(5),
                  pl.BlockSpec((rows, LANES), lambda i: (i, 0)),
                  pl.BlockSpec((None, GLA_GATE_RANK, GLA_QK_W), lambda i: (l, 0, 0)),
                  pl.BlockSpec((None, 1, GLA_QK_W), lambda i: (l, 0, 0)),
                  pl.BlockSpec((None, 1, GLA_DV), lambda i: (l, 0, 0)),
                  pl.BlockSpec(((GLA_LEVELS + 2) * L, L), lambda i: (0, 0))],
        out_specs=pl.BlockSpec((rows, GLA_V_W), lambda i: (i, 0)),
        out_shape=jax.ShapeDtypeStruct((SEQ, GLA_V_W), BF16),
        scratch_shapes=[pltpu.VMEM((GLA_HEADS, GLA_DV, GLA_DK), F32)],
        compiler_params=_params("arbitrary"),
        name="gla",
    )(p, p, p, p, p, p, a_cols, w_gla_a, b_gla_a.reshape(DEPTH, 1, GLA_QK_W),
      gla_out_gain.reshape(DEPTH, 1, GLA_DV), mst)


def _qk_gains(q_gain, k_gain, heads, q_scale):
    ones = jnp.ones((heads * HEAD_DIM,), F32)
    return jnp.concatenate([jnp.tile(q_gain * q_scale, heads), jnp.tile(k_gain, heads), ones]).reshape(1, -1)


def kernel(x, c, w_ada, b_ada, norm1_gain, norm2_gain, w_in, sb_q_gain, sb_k_gain, dil_q_gain, dil_k_gain,
           w_gla_a, b_gla_a, gla_out_gain, w_br_sb, w_br_dil, w_br_gla, w_out, w_ffn_in, w_ffn_out):
    assert x.shape == (1, SEQ, D_MODEL) and c.shape == (1, D_MODEL)
    xs = x.reshape(SEQ, D_MODEL)
    mod = _ada(c, w_ada, b_ada)
    u_tri = jnp.asarray(np.tril(np.ones((SB_T, SB_T), np.float32), -1), BF16)
    mst = jnp.asarray(_gla_sum_matrix(), BF16)
    scale = HEAD_DIM ** -0.5
    w_in_t = jnp.swapaxes(w_in, 1, 2)
    for l in range(DEPTH):
        h = _norm_mod(xs, norm1_gain, mod, l, 0, 1)
        p_sb = _inproj_qkv(h, w_in_t, l, OFF_SB, _qk_gains(sb_q_gain[l], sb_k_gain[l], SB_HEADS, scale * LOG2E),
                           BF16, "inproj_sb")
        p_dil = _inproj_qkv(h, w_in_t, l, OFF_DIL, _qk_gains(dil_q_gain[l], dil_k_gain[l], DIL_HEADS, scale),
                            F32, "inproj_dil")
        p = _inproj_plain(h, w_in_t, l, OFF_GLA, SEG_GLA_W, IN_TN, BF16, "inproj_gla")
        a_cols = _inproj_plain(h, w_in_t, l, OFF_A, LANES, LANES, F32, "inproj_a")
        gates = _inproj_gates(h, w_in_t, l)
        o_sb = _sb_attention(p_sb, u_tri)
        dil = [_dil_group(p_dil, g) for g in range(len(DIL_GROUPS))]
        o_dil = _dil_mix([d[0] for d in dil], [d[1] for d in dil])
        o_gla = _gla(p, a_cols, w_gla_a, b_gla_a, gla_out_gain, mst, l)
        y = _branch(o_sb, o_dil, o_gla, w_br_sb, w_br_dil, w_br_gla, gates, l)
        xs, h2 = _resid_proj_norm(y, w_out, xs, mod, l, 2, norm2_gain, 3, 4)
        u = _ffn_in(h2, w_ffn_in, l)
        xs = _resid_proj(u, w_ffn_out, xs, mod, l, 5, 512, 512)
    return xs.reshape(1, SEQ, D_MODEL)
```

```python
import functools
import math

import numpy as np
import jax
import jax.numpy as jnp
from jax import lax
from jax.experimental import pallas as pl
from jax.experimental.pallas import tpu as pltpu

F32 = jnp.float32
BF16 = jnp.bfloat16

D_MODEL = 2048
SEQ = 8192
DEPTH = 2
HEAD_DIM = 128
SB_HEADS = 6
DIL_GROUPS = ((128, 1), (512, 4), (2048, 16))
DIL_HEADS_PER_GROUP = 2
DIL_HEADS = DIL_HEADS_PER_GROUP * len(DIL_GROUPS)
GLA_HEADS = 4
GLA_DK = 128
GLA_DV = 256
GLA_GATE_RANK = 16
GLA_GATE_TAU = 16.0
N_BRANCH = 3
D_FF = 5632
N_MOD = 6
EPS = 1e-6

SB_W = SB_HEADS * HEAD_DIM
DIL_W = DIL_HEADS * HEAD_DIM
DIL_OUT_W = DIL_HEADS_PER_GROUP * HEAD_DIM
GLA_QK_W = GLA_HEADS * GLA_DK
GLA_V_W = GLA_HEADS * GLA_DV
OFF_SB = 0
OFF_DIL = 3 * SB_W
OFF_GLA = OFF_DIL + 3 * DIL_W
OFF_A = OFF_GLA + 2 * GLA_QK_W + 2 * GLA_V_W
OFF_GATE = OFF_A + GLA_GATE_RANK
SEG_SB_W = 3 * SB_W
SEG_DIL_W = 3 * DIL_W
SEG_GLA_W = 2 * GLA_QK_W + 2 * GLA_V_W

LANES = 128
VMEM_LIMIT = 56 * 1024 * 1024

MM_TM = 1024
IN_TN = 768
SB_T = 256
SB_G = 6
SB_QS = 1
LOG2E = 1.4426950408889634
SB_DEAD = -160.0
DIL_BLK = 128
GLA_L = 128
GLA_NB = 4
GLA_LEVELS = 7


def _params(*sem):
    return pltpu.CompilerParams(dimension_semantics=sem, vmem_limit_bytes=VMEM_LIMIT)


def _dot(a, b):
    return jnp.dot(a, b, preferred_element_type=F32)


def _dot_nt(a, b):
    return lax.dot_general(a, b, (((1,), (1,)), ((), ())), preferred_element_type=F32)


def _split2(x):
    hi = x.astype(BF16)
    lo = (x - hi.astype(F32)).astype(BF16)
    return hi, lo


def _split3(x):
    p1 = x.astype(BF16)
    r1 = x - p1.astype(F32)
    p2 = r1.astype(BF16)
    p3 = (r1 - p2.astype(F32)).astype(BF16)
    return p1, p2, p3


def _sigmoid(x):
    return 0.5 * jnp.tanh(0.5 * x) + 0.5


def _log_sigmoid(z):
    return jnp.minimum(z, 0.0) - jnp.log(1.0 + jnp.exp(-jnp.abs(z)))


def _ada_kernel(c_ref, w_ref, b_ref, o_ref):
    c = c_ref[...]
    s = c * jax.nn.sigmoid(c)
    o_ref[...] = jnp.sum(w_ref[...] * s, axis=0, keepdims=True) + b_ref[...]


def _ada(c, w_ada, b_ada):
    tn = 1024
    n = N_MOD * D_MODEL
    return pl.pallas_call(
        _ada_kernel,
        grid=(DEPTH, n // tn),
        in_specs=[
            pl.BlockSpec((D_MODEL, 1), lambda l, j: (0, 0)),
            pl.BlockSpec((None, D_MODEL, tn), lambda l, j: (l, 0, j)),
            pl.BlockSpec((None, 1, tn), lambda l, j: (l, 0, j)),
        ],
        out_specs=pl.BlockSpec((None, 1, tn), lambda l, j: (l, 0, j)),
        out_shape=jax.ShapeDtypeStruct((DEPTH, 1, n), F32),
        compiler_params=_params("arbitrary", "arbitrary"),
        name="ada_mod",
    )(c.reshape(D_MODEL, 1), w_ada, b_ada.reshape(DEPTH, 1, n))


def _norm_kernel(x_ref, g_ref, sc_ref, sh_ref, o_ref):
    x = x_ref[...]
    ms = jnp.mean(x * x, axis=-1, keepdims=True)
    y = x * lax.rsqrt(ms + EPS) * g_ref[...]
    o_ref[...] = (y * (1.0 + sc_ref[...]) + sh_ref[...]).astype(BF16)


def _norm_mod(x, gain, mod, l, shift_idx, scale_idx):
    tm = MM_TM
    return pl.pallas_call(
        _norm_kernel,
        grid=(SEQ // tm,),
        in_specs=[
            pl.BlockSpec((tm, D_MODEL), lambda i: (i, 0)),
            pl.BlockSpec((None, 1, D_MODEL), lambda i: (l, 0, 0)),
            pl.BlockSpec((None, 1, D_MODEL), lambda i: (l, 0, scale_idx)),
            pl.BlockSpec((None, 1, D_MODEL), lambda i: (l, 0, shift_idx)),
        ],
        out_specs=pl.BlockSpec((tm, D_MODEL), lambda i: (i, 0)),
        out_shape=jax.ShapeDtypeStruct((SEQ, D_MODEL), BF16),
        compiler_params=_params("arbitrary"),
        name="norm_mod",
    )(x, gain.reshape(DEPTH, 1, D_MODEL), mod, mod)


def _cast_weights(pairs):
    @pl.when(pl.program_id(1) == 0)
    def _():
        for w_ref, wb_ref in pairs:
            wb_ref[...] = w_ref[...].astype(BF16)


MXU_N = 256


def _col_chunks(tn):
    return [slice(c, min(c + MXU_N, tn)) for c in range(0, tn, MXU_N)]


XPOSE_CHUNK = 256


def _cast_weights_t(wt_ref, wb_ref, skip=0, tail_ref=None):
    @pl.when(pl.program_id(1) == 0)
    def _():
        for c in range(D_MODEL // XPOSE_CHUNK):
            cs = slice(c * XPOSE_CHUNK, (c + 1) * XPOSE_CHUNK)
            blk = wt_ref[skip:, cs]
            if tail_ref is not None:
                blk = jnp.concatenate([blk, tail_ref[:, cs]], axis=0)
            wb_ref[cs, :] = blk.T.astype(BF16)


def _inproj_qkv_kernel(a_ref, w_ref, g_ref, o_ref, wb_ref):
    _cast_weights_t(w_ref, wb_ref)
    is_norm = pl.program_id(0) < 2
    a = a_ref[...]

    @pl.when(is_norm)
    def _():
        g = g_ref[...]
        for ns in _col_chunks(IN_TN):
            acc = _dot(a, wb_ref[:, ns])
            for c in range(MXU_N // HEAD_DIM):
                cs = slice(ns.start + c * HEAD_DIM, ns.start + (c + 1) * HEAD_DIM)
                blk = acc[:, c * HEAD_DIM:(c + 1) * HEAD_DIM]
                ms = jnp.mean(blk * blk, axis=-1, keepdims=True)
                o_ref[:, cs] = (blk * lax.rsqrt(ms + EPS) * g[:, cs]).astype(o_ref.dtype)

    @pl.when(jnp.logical_not(is_norm))
    def _():
        for ns in _col_chunks(IN_TN):
            o_ref[:, ns] = _dot(a, wb_ref[:, ns]).astype(o_ref.dtype)


def _inproj_qkv(h, w_in_t, l, col0, gains, out_dtype, name):
    tm, tn = MM_TM, IN_TN
    width = 3 * tn
    return pl.pallas_call(
        _inproj_qkv_kernel,
        grid=(width // tn, SEQ // tm),
        in_specs=[
            pl.BlockSpec((tm, D_MODEL), lambda j, m: (m, 0)),
            pl.BlockSpec((None, tn, D_MODEL), lambda j, m: (l, col0 // tn + j, 0)),
            pl.BlockSpec((1, tn), lambda j, m: (0, j)),
        ],
        out_specs=pl.BlockSpec((tm, tn), lambda j, m: (m, j)),
        out_shape=jax.ShapeDtypeStruct((SEQ, width), out_dtype),
        scratch_shapes=[pltpu.VMEM((D_MODEL, tn), BF16)],
        compiler_params=_params("arbitrary", "arbitrary"),
        name=name,
    )(h, w_in_t, gains)


def _plain_kernel(a_ref, w_ref, o_ref, wb_ref):
    _cast_weights_t(w_ref, wb_ref)
    a = a_ref[...]
    for ns in _col_chunks(o_ref.shape[1]):
        o_ref[:, ns] = _dot(a, wb_ref[:, ns]).astype(o_ref.dtype)


def _inproj_plain(h, w_in_t, l, col0, width, tn, out_dtype, name):
    tm = MM_TM
    return pl.pallas_call(
        _plain_kernel,
        grid=(width // tn, SEQ // tm),
        in_specs=[
            pl.BlockSpec((tm, D_MODEL), lambda j, m: (m, 0)),
            pl.BlockSpec((None, tn, D_MODEL), lambda j, m: (l, col0 // tn + j, 0)),
        ],
        out_specs=pl.BlockSpec((tm, tn), lambda j, m: (m, j)),
        out_shape=jax.ShapeDtypeStruct((SEQ, width), out_dtype),
        scratch_shapes=[pltpu.VMEM((D_MODEL, tn), BF16)],
        compiler_params=_params("arbitrary", "arbitrary"),
        name=name,
    )(h, w_in_t)


GATE_SHIFT = OFF_GATE - OFF_A


def _gates_kernel(a_ref, wm_ref, wx_ref, o_ref, wb_ref):
    _cast_weights_t(wm_ref, wb_ref, skip=GATE_SHIFT, tail_ref=wx_ref)
    a = a_ref[...]
    for ns in _col_chunks(IN_TN):
        o_ref[:, ns] = _sigmoid(_dot(a, wb_ref[:, ns])).astype(BF16)


def _inproj_gates(h, w_in_t, l):
    tm, tn = MM_TM, IN_TN
    n = N_BRANCH * D_MODEL
    return pl.pallas_call(
        _gates_kernel,
        grid=(n // tn, SEQ // tm),
        in_specs=[
            pl.BlockSpec((tm, D_MODEL), lambda j, m: (m, 0)),
            pl.BlockSpec((None, tn, D_MODEL), lambda j, m: (l, OFF_A // tn + j, 0)),
            pl.BlockSpec((None, GATE_SHIFT, D_MODEL), lambda j, m: (l, (OFF_A + (j + 1) * tn) // GATE_SHIFT, 0)),
        ],
        out_specs=pl.BlockSpec((tm, tn), lambda j, m: (m, j)),
        out_shape=jax.ShapeDtypeStruct((SEQ, n), BF16),
        scratch_shapes=[pltpu.VMEM((D_MODEL, tn), BF16)],
        compiler_params=_params("arbitrary", "arbitrary"),
        name="inproj_gates",
    )(h, w_in_t, w_in_t)


def _branch_kernel(a0, a1, a2, w0, w1, w2, g0, g1, g2, o_ref, wb0, wb1, wb2):
    _cast_weights([(w0, wb0), (w1, wb1), (w2, wb2)])
    x0, x1, x2 = a0[...], a1[...], a2[...]
    for ns in _col_chunks(o_ref.shape[1]):
        y = g0[:, ns].astype(F32) * _dot(x0, wb0[:, ns])
        y = y + g1[:, ns].astype(F32) * _dot(x1, wb1[:, ns])
        y = y + g2[:, ns].astype(F32) * _dot(x2, wb2[:, ns])
        o_ref[:, ns] = y.astype(BF16)


def _branch(o_sb, o_dil, o_gla, w_br_sb, w_br_dil, w_br_gla, gates, l):
    tm, tn = MM_TM, 1024
    nb = D_MODEL // tn
    a_spec = lambda w: pl.BlockSpec((tm, w), lambda j, m: (m, 0))
    w_spec = lambda w: pl.BlockSpec((None, w, tn), lambda j, m: (l, 0, j))
    g_spec = lambda b: pl.BlockSpec((tm, tn), lambda j, m: (m, b * nb + j))
    return pl.pallas_call(
        _branch_kernel,
        grid=(nb, SEQ // tm),
        in_specs=[a_spec(SB_W), a_spec(DIL_OUT_W), a_spec(GLA_V_W),
                  w_spec(SB_W), w_spec(DIL_OUT_W), w_spec(GLA_V_W),
                  g_spec(0), g_spec(1), g_spec(2)],
        out_specs=pl.BlockSpec((tm, tn), lambda j, m: (m, j)),
        out_shape=jax.ShapeDtypeStruct((SEQ, D_MODEL), BF16),
        scratch_shapes=[pltpu.VMEM((SB_W, tn), BF16), pltpu.VMEM((DIL_OUT_W, tn), BF16),
                        pltpu.VMEM((GLA_V_W, tn), BF16)],
        compiler_params=_params("arbitrary", "arbitrary"),
        name="branch_proj",
    )(o_sb, o_dil, o_gla, w_br_sb, w_br_dil, w_br_gla, gates, gates, gates)


def _resid_kernel(a_ref, w_ref, x_ref, gate_ref, o_ref, wb_ref):
    _cast_weights([(w_ref, wb_ref)])
    a = a_ref[...]
    for ns in _col_chunks(o_ref.shape[1]):
        o_ref[:, ns] = x_ref[:, ns] + gate_ref[:, ns] * _dot(a, wb_ref[:, ns])


def _resid_proj(a, w, x, mod, l, gate_idx, tm, tn):
    k = a.shape[1]
    nb = D_MODEL // tn
    return pl.pallas_call(
        _resid_kernel,
        grid=(nb, SEQ // tm),
        in_specs=[
            pl.BlockSpec((tm, k), lambda j, m: (m, 0)),
            pl.BlockSpec((None, k, tn), lambda j, m: (l, 0, j)),
            pl.BlockSpec((tm, tn), lambda j, m: (m, j)),
            pl.BlockSpec((None, 1, tn), lambda j, m: (l, 0, gate_idx * nb + j)),
        ],
        out_specs=pl.BlockSpec((tm, tn), lambda j, m: (m, j)),
        out_shape=jax.ShapeDtypeStruct((SEQ, D_MODEL), F32),
        scratch_shapes=[pltpu.VMEM((k, tn), BF16)],
        compiler_params=_params("arbitrary", "arbitrary"),
        name="resid_proj",
    )(a, w, x, mod)


def _resid_norm_kernel(a_ref, w_ref, x_ref, gate_ref, g_ref, sc_ref, sh_ref, o_ref, h_ref, wb_ref):
    @pl.when(pl.program_id(0) == 0)
    def _():
        wb_ref[...] = w_ref[...].astype(BF16)

    a = a_ref[...]
    for ns in _col_chunks(D_MODEL):
        o_ref[:, ns] = x_ref[:, ns] + gate_ref[:, ns] * _dot(a, wb_ref[:, ns])
    x = o_ref[...]
    ms = jnp.mean(x * x, axis=-1, keepdims=True)
    y = x * lax.rsqrt(ms + EPS) * g_ref[...]
    h_ref[...] = (y * (1.0 + sc_ref[...]) + sh_ref[...]).astype(BF16)


def _resid_proj_norm(a, w, x, mod, l, gate_idx, gain, shift_idx, scale_idx):
    tm = 512
    k = a.shape[1]
    row = lambda idx: pl.BlockSpec((None, 1, D_MODEL), lambda m: (l, 0, idx))
    return pl.pallas_call(
        _resid_norm_kernel,
        grid=(SEQ // tm,),
        in_specs=[
            pl.BlockSpec((tm, k), lambda m: (m, 0)),
            pl.BlockSpec((None, k, D_MODEL), lambda m: (l, 0, 0), pipeline_mode=pl.Buffered(1)),
            pl.BlockSpec((tm, D_MODEL), lambda m: (m, 0)),
            row(gate_idx), row(0), row(scale_idx), row(shift_idx),
        ],
        out_specs=[pl.BlockSpec((tm, D_MODEL), lambda m: (m, 0)), pl.BlockSpec((tm, D_MODEL), lambda m: (m, 0))],
        out_shape=[jax.ShapeDtypeStruct((SEQ, D_MODEL), F32), jax.ShapeDtypeStruct((SEQ, D_MODEL), BF16)],
        scratch_shapes=[pltpu.VMEM((k, D_MODEL), BF16)],
        compiler_params=_params("arbitrary"),
        name="resid_proj_norm",
    )(a, w, x, mod, gain.reshape(DEPTH, 1, D_MODEL), mod, mod)


def _ffn_in_kernel(a_ref, wg_ref, wu_ref, o_ref, wgb_ref, wub_ref):
    _cast_weights([(wg_ref, wgb_ref), (wu_ref, wub_ref)])
    a = a_ref[...]
    for ns in _col_chunks(o_ref.shape[1]):
        g = _dot(a, wgb_ref[:, ns])
        u = _dot(a, wub_ref[:, ns])
        o_ref[:, ns] = (g * _sigmoid(g) * u).astype(BF16)


def _ffn_in(h, w_ffn_in, l):
    tm, tn = MM_TM, 512
    nb = D_FF // tn
    return pl.pallas_call(
        _ffn_in_kernel,
        grid=(nb, SEQ // tm),
        in_specs=[
            pl.BlockSpec((tm, D_MODEL), lambda j, m: (m, 0)),
            pl.BlockSpec((None, D_MODEL, tn), lambda j, m: (l, 0, j)),
            pl.BlockSpec((None, D_MODEL, tn), lambda j, m: (l, 0, nb + j)),
        ],
        out_specs=pl.BlockSpec((tm, tn), lambda j, m: (m, j)),
        out_shape=jax.ShapeDtypeStruct((SEQ, D_FF), BF16),
        scratch_shapes=[pltpu.VMEM((D_MODEL, tn), BF16), pltpu.VMEM((D_MODEL, tn), BF16)],
        compiler_params=_params("arbitrary", "arbitrary"),
        name="ffn_in",
    )(h, w_ffn_in, w_ffn_in)


def _sb_kernel(q_ref, k_ref, v_ref, u_ref, o_ref, z_ref, acc_ref, carry_ref):
    t = SB_T
    i = pl.program_id(1)
    u = u_ref[...]
    units = [(s, g) for s in range(SB_QS) for g in range(SB_G)]

    def q_rows(s):
        return slice(s * t, (s + 1) * t)

    def cols(g):
        return slice(g * HEAD_DIM, (g + 1) * HEAD_DIM)

    def logits_into(kb, slot, subs):
        ks = pl.multiple_of(kb * t, t)
        for n, (s, g) in enumerate(units):
            if s in subs:
                z_ref[slot, n] = _dot_nt(q_ref[q_rows(s), cols(g)], k_ref[pl.ds(ks, t), cols(g)])

    def tile(kb, slot, kinds):
        ks = pl.multiple_of(kb * t, t)
        if 'diag' in kinds:
            row = lax.broadcasted_iota(jnp.int32, (t, t), 0)
            col = lax.broadcasted_iota(jnp.int32, (t, t), 1)
            causal = col < row
        for n, (s, g) in enumerate(units):
            if kinds[s] is None:
                continue
            diag = kinds[s] == 'diag'
            z = z_ref[slot, n]
            ls = jnp.minimum(z, 0.0) - jnp.log(1.0 + jnp.exp2(-jnp.abs(z))) * LOG2E
            lr = ls - z
            if diag:
                lr = jnp.where(causal, lr, 0.0)
            carry = carry_ref[n]
            after = _dot(lr.astype(BF16), u) + jnp.concatenate([carry] * (t // LANES), axis=1)
            w = jnp.exp2(ls + after)
            if diag:
                w = jnp.where(causal, w, 0.0)
            acc_ref[n] += _dot(w.astype(BF16), v_ref[pl.ds(ks, t), cols(g)])
            carry_ref[n] = carry + jnp.sum(lr, axis=-1, keepdims=True)

    acc_ref[...] = jnp.zeros_like(acc_ref)
    carry_ref[...] = jnp.zeros_like(carry_ref)
    all_subs = tuple(range(SB_QS))
    top = SB_QS * i + SB_QS - 1
    logits_into(top, 0, (SB_QS - 1,))
    for d in range(SB_QS):
        kinds = [None if s < SB_QS - 1 - d else ('diag' if s == SB_QS - 1 - d else 'full') for s in all_subs]
        tile(top - d, d % 2, kinds)
        nxt = tuple(s for s in all_subs if s >= SB_QS - 2 - d) if d < SB_QS - 1 else all_subs
        logits_into(jnp.maximum(top - d - 1, 0), (d + 1) % 2, nxt)

    def top_carry():
        return jnp.max(carry_ref[...])

    def cond(c):
        j, top = c
        return jnp.logical_and(j < SB_QS * i, top > SB_DEAD)

    def body(c):
        j, _ = c
        kb = SB_QS * i - 1 - j
        tile(kb, (j + SB_QS) % 2, ['full'] * SB_QS)
        logits_into(jnp.maximum(kb - 1, 0), (j + SB_QS + 1) % 2, all_subs)
        return j + 1, top_carry()

    lax.while_loop(cond, body, (jnp.int32(0), top_carry()))
    for n, (s, g) in enumerate(units):
        o_ref[q_rows(s), cols(g)] = acc_ref[n].astype(BF16)


def _sb_attention(p, u_tri):
    t = SB_T
    tq = SB_QS * t
    w = SB_G * HEAD_DIM
    ng = SB_HEADS // SB_G
    nu = SB_QS * SB_G
    return pl.pallas_call(
        _sb_kernel,
        grid=(ng, SEQ // tq),
        in_specs=[
            pl.BlockSpec((tq, w), lambda h, i: (i, h)),
            pl.BlockSpec((SEQ, w), lambda h, i: (0, ng + h), pipeline_mode=pl.Buffered(1)),
            pl.BlockSpec((SEQ, w), lambda h, i: (0, 2 * ng + h), pipeline_mode=pl.Buffered(1)),
            pl.BlockSpec((t, t), lambda h, i: (0, 0)),
        ],
        out_specs=pl.BlockSpec((tq, w), lambda h, i: (i, h)),
        out_shape=jax.ShapeDtypeStruct((SEQ, SB_W), BF16),
        scratch_shapes=[pltpu.VMEM((2, nu, t, t), F32), pltpu.VMEM((nu, t, HEAD_DIM), F32),
                        pltpu.VMEM((nu, t, LANES), F32)],
        compiler_params=_params("arbitrary", "arbitrary"),
        name="sb_attention",
    )(p, p, p, u_tri)


DIL_ROWS = 2048


def _stream_rows(start, dilation):
    if dilation == 1:
        return pl.ds(start, DIL_BLK)
    return pl.ds(start, DIL_BLK, stride=dilation)


def _dil_kernel(*refs, dilation, slopes):
    nh = DIL_HEADS_PER_GROUP
    ins = [refs[5 * hh:5 * hh + 5] for hh in range(nh)]
    o_refs = refs[5 * nh:6 * nh]
    l_refs = refs[6 * nh:7 * nh]
    n = pl.program_id(0)
    blk, r = DIL_BLK, dilation
    i_idx = lax.broadcasted_iota(jnp.int32, (blk, 2 * blk), 0)
    j_idx = lax.broadcasted_iota(jnp.int32, (blk, 2 * blk), 1)
    delta = blk + i_idx - j_idx
    in_window = (delta >= 0) & (delta <= blk)
    first_valid = in_window & (j_idx >= jnp.where(n > 0, 0, blk))
    delta_f = delta.astype(F32)
    biases = [(-slopes[hh] * r) * delta_f for hh in range(DIL_HEADS_PER_GROUP)]
    for rho in range(r):
        for b in range(DIL_ROWS // (r * blk)):
            cur = _stream_rows(rho + r * b * blk, r)
            prev = _stream_rows(rho + r * max(b - 1, 0) * blk, r)
            valid = first_valid if b == 0 else in_window
            for hh in range(nh):
                q_ref, kc_ref, kp_ref, vc_ref, vp_ref = ins[hh]
                k_prev = kp_ref[prev, :] if b == 0 else kc_ref[prev, :]
                v_prev = vp_ref[prev, :] if b == 0 else vc_ref[prev, :]
                kk = jnp.concatenate([k_prev, kc_ref[cur, :]], axis=0).astype(BF16)
                vv = jnp.concatenate([v_prev, vc_ref[cur, :]], axis=0).astype(BF16)
                s = _dot_nt(q_ref[cur, :].astype(BF16), kk)
                logits = jnp.where(valid, s + biases[hh], -jnp.inf)
                m = jnp.max(logits, axis=-1, keepdims=True)
                pexp = jnp.exp(logits - m)
                den = jnp.sum(pexp, axis=-1, keepdims=True)
                o_refs[hh][cur, :] = _dot(pexp.astype(BF16), vv) / den
                l_refs[hh][cur, :] = jnp.broadcast_to(m + jnp.log(den), (blk, HEAD_DIM))


def _dil_group(p_dil, g):
    _, r = DIL_GROUPS[g]
    prev_rows = r * DIL_BLK
    ratio = DIL_ROWS // prev_rows
    nh = DIL_HEADS_PER_GROUP
    slopes = tuple(2.0 ** (-8.0 * (g * nh + hh + 1) / DIL_HEADS) for hh in range(nh))
    cur = lambda part, hh: pl.BlockSpec((DIL_ROWS, HEAD_DIM), lambda n: (n, part * DIL_HEADS + g * nh + hh))
    prev = lambda part, hh: pl.BlockSpec((prev_rows, HEAD_DIM),
                                         lambda n: (jnp.maximum(n * ratio - 1, 0), part * DIL_HEADS + g * nh + hh))
    in_specs = []
    for hh in range(nh):
        in_specs += [cur(0, hh), cur(1, hh), prev(1, hh), cur(2, hh), prev(2, hh)]
    out_spec = pl.BlockSpec((DIL_ROWS, HEAD_DIM), lambda n: (n, 0))
    res = pl.pallas_call(
        functools.partial(_dil_kernel, dilation=r, slopes=slopes),
        grid=(SEQ // DIL_ROWS,),
        in_specs=in_specs,
        out_specs=[out_spec] * (2 * nh),
        out_shape=[jax.ShapeDtypeStruct((SEQ, HEAD_DIM), F32)] * (2 * nh),
        compiler_params=_params("arbitrary"),
        name=f"dil_group{g}",
    )(*([p_dil] * (5 * nh)))
    return res[:nh], res[nh:]


def _dil_mix_kernel(*refs):
    ng, nh = len(DIL_GROUPS), DIL_HEADS_PER_GROUP
    out_ref = refs[-1]
    for hh in range(nh):
        o = [refs[g * nh + hh][...] for g in range(ng)]
        lse = [refs[ng * nh + g * nh + hh][...] for g in range(ng)]
        m = functools.reduce(jnp.maximum, lse)
        e = [jnp.exp(x - m) for x in lse]
        num = functools.reduce(lambda a, b: a + b, [ei * oi for ei, oi in zip(e, o)])
        den = functools.reduce(lambda a, b: a + b, e)
        out_ref[:, hh * HEAD_DIM:(hh + 1) * HEAD_DIM] = (num / den).astype(BF16)


def _dil_mix(outs, lses):
    tm = 1024
    flat = [a for grp in outs for a in grp] + [a for grp in lses for a in grp]
    return pl.pallas_call(
        _dil_mix_kernel,
        grid=(SEQ // tm,),
        in_specs=[pl.BlockSpec((tm, HEAD_DIM), lambda i: (i, 0))] * len(flat),
        out_specs=pl.BlockSpec((tm, DIL_OUT_W), lambda i: (i, 0)),
        out_shape=jax.ShapeDtypeStruct((SEQ, DIL_OUT_W), BF16),
        compiler_params=_params("arbitrary"),
        name="dil_mix",
    )(*flat)


def _gla_sum_matrix():
    L = GLA_L
    t = np.arange(L)[:, None]
    s = np.arange(L)[None, :]
    mats = []
    for lev in range(GLA_LEVELS):
        size = 2 << lev
        half = size // 2
        mid = (t // size) * size + half
        later = (t % size) >= half
        mats.append(np.where(later, (s >= mid) & (s <= t), (s > t) & (s < mid)))
    mats.append(s <= t)
    mats.append(s > t)
    return np.concatenate(mats, axis=0).astype(np.float32)


def _gla_kernel(q_ref, k_ref, v0_ref, v1_ref, r0_ref, r1_ref, a_ref, wa_ref, ba_ref, gain_ref, mst_ref,
                o_ref, st_ref):
    L = GLA_L

    @pl.when(pl.program_id(0) == 0)
    def _():
        st_ref[...] = jnp.zeros_like(st_ref)

    row = lax.broadcasted_iota(jnp.int32, (L, L), 0)
    col = lax.broadcasted_iota(jnp.int32, (L, L), 1)
    mst = mst_ref[...]
    gain = gain_ref[...]
    for blk, h in [(b, hh) for b in range(GLA_NB) for hh in range(GLA_HEADS)]:
        rs = slice(blk * L, (blk + 1) * L)
        ks = slice(h * GLA_DK, (h + 1) * GLA_DK)
        a_hi, a_lo = _split2(a_ref[rs, 0:GLA_GATE_RANK])
        wa_hi, wa_lo = _split2(wa_ref[:, ks])
        a = _dot(a_hi, wa_hi) + _dot(a_lo, wa_hi) + _dot(a_hi, wa_lo) + ba_ref[:, ks]
        log_a = _log_sigmoid(a) * (1.0 / GLA_GATE_TAU)
        e_all = _dot(mst, log_a.astype(BF16))
        qb = q_ref[rs, ks]
        kb = k_ref[rs, ks]
        q = qb.astype(F32)
        k = kb.astype(F32)
        scores = jnp.where(row == col, _dot_nt(qb, kb), 0.0)
        for lev in range(GLA_LEVELS):
            size = 2 << lev
            e = jnp.exp(e_all[lev * L:(lev + 1) * L])
            later = (row & (size // 2)) != 0
            qt = jnp.where(later, q * e, 0.0).astype(BF16)
            kt = jnp.where(later, 0.0, k * e).astype(BF16)
            part = _dot_nt(qt, kt)
            if size < L:
                part = jnp.where((row >> (lev + 1)) == (col >> (lev + 1)), part, 0.0)
            scores = scores + part
        e_cum = e_all[GLA_LEVELS * L:(GLA_LEVELS + 1) * L]
        e_rev = e_all[(GLA_LEVELS + 1) * L:(GLA_LEVELS + 2) * L]
        total = e_cum[L - 1:L]
        q_in = (q * jnp.exp(e_cum)).astype(BF16)
        k_out = (k * jnp.exp(e_rev)).astype(BF16)
        v_src = v0_ref if h < 2 else v1_ref
        r_src = r0_ref if h < 2 else r1_ref
        vs = slice((h % 2) * GLA_DV, (h % 2 + 1) * GLA_DV)
        v = v_src[rs, vs]
        st = st_ref[h]
        o = _dot_nt(q_in, st.astype(BF16)) + _dot(scores.astype(BF16), v)
        o = o * (GLA_DK ** -0.5)
        v_t = v.astype(F32).T.astype(BF16)
        st_ref[h] = st * jnp.exp(total) + _dot(v_t, k_out)
        ms = jnp.mean(o * o, axis=-1, keepdims=True)
        y = o * lax.rsqrt(ms + EPS) * gain
        r = r_src[rs, vs].astype(F32)
        o_ref[rs, h * GLA_DV:(h + 1) * GLA_DV] = (y * (r * _sigmoid(r))).astype(BF16)


def _gla(p, a_cols, w_gla_a, b_gla_a, gla_out_gain, mst, l):
    L = GLA_L
    rows = GLA_NB * L
    w = 512
    col = lambda c: pl.BlockSpec((rows, w), lambda i: (i, c))
    return pl.pallas_call(
        _gla_kernel,
        grid=(SEQ // rows,),
        in_specs=[col(0), col(1), col(2), col(3), col(4), col(5),
                  pl.BlockSpec((rows, LANES), lambda i: (i, 0)),
                  pl.BlockSpec((None, GLA_GATE_RANK, GLA_QK_W), lambda i: (l, 0, 0)),
                  pl.BlockSpec((None, 1, GLA_QK_W), lambda i: (l, 0, 0)),
                  pl.BlockSpec((None, 1, GLA_DV), lambda i: (l, 0, 0)),
                  pl.BlockSpec(((GLA_LEVELS + 2) * L, L), lambda i: (0, 0))],
        out_specs=pl.BlockSpec((rows, GLA_V_W), lambda i: (i, 0)),
        out_shape=jax.ShapeDtypeStruct((SEQ, GLA_V_W), BF16),
        scratch_shapes=[pltpu.VMEM((GLA_HEADS, GLA_DV, GLA_DK), F32)],
        compiler_params=_params("arbitrary"),
        name="gla",
    )(p, p, p, p, p, p, a_cols, w_gla_a, b_gla_a.reshape(DEPTH, 1, GLA_QK_W),
      gla_out_gain.reshape(DEPTH, 1, GLA_DV), mst)


def _qk_gains(q_gain, k_gain, heads, q_scale):
    ones = jnp.ones((heads * HEAD_DIM,), F32)
    return jnp.concatenate([jnp.tile(q_gain * q_scale, heads), jnp.tile(k_gain, heads), ones]).reshape(1, -1)


def kernel(x, c, w_ada, b_ada, norm1_gain, norm2_gain, w_in, sb_q_gain, sb_k_gain, dil_q_gain, dil_k_gain,
           w_gla_a, b_gla_a, gla_out_gain, w_br_sb, w_br_dil, w_br_gla, w_out, w_ffn_in, w_ffn_out):
    assert x.shape == (1, SEQ, D_MODEL) and c.shape == (1, D_MODEL)
    xs = x.reshape(SEQ, D_MODEL)
    mod = _ada(c, w_ada, b_ada)
    u_tri = jnp.asarray(np.tril(np.ones((SB_T, SB_T), np.float32), -1), BF16)
    mst = jnp.asarray(_gla_sum_matrix(), BF16)
    scale = HEAD_DIM ** -0.5
    w_in_t = jnp.swapaxes(w_in, 1, 2)
    for l in range(DEPTH):
        h = _norm_mod(xs, norm1_gain, mod, l, 0, 1)
        p_sb = _inproj_qkv(h, w_in_t, l, OFF_SB, _qk_gains(sb_q_gain[l], sb_k_gain[l], SB_HEADS, scale * LOG2E),
                           BF16, "inproj_sb")
        p_dil = _inproj_qkv(h, w_in_t, l, OFF_DIL, _qk_gains(dil_q_gain[l], dil_k_gain[l], DIL_HEADS, scale),
                            F32, "inproj_dil")
        p = _inproj_plain(h, w_in_t, l, OFF_GLA, SEG_GLA_W, IN_TN, BF16, "inproj_gla")
        a_cols = _inproj_plain(h, w_in_t, l, OFF_A, LANES, LANES, F32, "inproj_a")
        gates = _inproj_gates(h, w_in_t, l)
        o_sb = _sb_attention(p_sb, u_tri)
        dil = [_dil_group(p_dil, g) for g in range(len(DIL_GROUPS))]
        o_dil = _dil_mix([d[0] for d in dil], [d[1] for d in dil])
        o_gla = _gla(p, a_cols, w_gla_a, b_gla_a, gla_out_gain, mst, l)
        y = _branch(o_sb, o_dil, o_gla, w_br_sb, w_br_dil, w_br_gla, gates, l)
        xs, h2 = _resid_proj_norm(y, w_out, xs, mod, l, 2, norm2_gain, 3, 4)
        u = _ffn_in(h2, w_ffn_in, l)
        xs = _resid_proj(u, w_ffn_out, xs, mod, l, 5, 512, 512)
    return xs.reshape(1, SEQ, D_MODEL)
```
